```python
import jax, jax.numpy as jnp
from jax import lax
import numpy as np

D_MODEL = 2048
BATCH = 4
SEQ = 4096
DEPTH = 2
DEC_BATCH = 16
DEC_SEQ = 16
PAST_LEN = 2048

CHUNK = 64
Q_BLOCK = 128
MLA_HEADS = 8
QK_NOPE = 128
QK_ROPE = 64
V_HEAD = 128
Q_LORA = 512
KV_LORA = 512
MLA_QK = QK_NOPE + QK_ROPE
MLA_SCALE = MLA_QK ** -0.5
RET_HEADS = 8
RET_DK = 128
RET_DV = 256
RET_QK_W = RET_HEADS * RET_DK
RET_V_W = RET_HEADS * RET_DV
RET_K_SCALE = RET_DK ** -0.5
D_FF = 5632
ROPE_THETA = 10000.0
NORM_EPS = 1e-6
GN_EPS = 1e-5
IN_WIDTH = Q_LORA + KV_LORA + QK_ROPE + 2 * RET_QK_W + 2 * RET_V_W + 2 * D_MODEL

kernel_name = "mla_retention_gated_macaron_stream_step"


def _in_offsets():
    sizes = (Q_LORA, KV_LORA, QK_ROPE, RET_QK_W, RET_QK_W, RET_V_W, RET_V_W, D_MODEL, D_MODEL)
    offs, acc = [], 0
    for s in sizes[:-1]:
        acc += s
        offs.append(acc)
    return offs


def rms_norm(x, g):
    xf = x.astype(jnp.float32)
    y = xf * lax.rsqrt(jnp.mean(xf * xf, axis=-1, keepdims=True) + NORM_EPS)
    return (y * g.astype(jnp.float32)).astype(x.dtype)


def head_group_norm(x):
    xf = x.astype(jnp.float32)
    mu = jnp.mean(xf, axis=-1, keepdims=True)
    var = jnp.mean(jnp.square(xf - mu), axis=-1, keepdims=True)
    return (xf - mu) * lax.rsqrt(var + GN_EPS)


def swiglu(x, w13, w2):
    a, b = jnp.split(x @ w13, 2, axis=-1)
    return (jax.nn.silu(a) * b) @ w2


def rope(x, pos):
    d = x.shape[-1]
    inv = ROPE_THETA ** (-jnp.arange(0, d, 2, dtype=jnp.float32) / d)
    ang = pos.astype(jnp.float32)[:, None] * inv[None, :]
    shape = (pos.shape[0],) + (1,) * (x.ndim - 3) + (d // 2,)
    cos = jnp.cos(ang).reshape(shape)
    sin = jnp.sin(ang).reshape(shape)
    xf = x.astype(jnp.float32)
    x1, x2 = xf[..., : d // 2], xf[..., d // 2:]
    return jnp.concatenate([x1 * cos - x2 * sin, x2 * cos + x1 * sin], axis=-1).astype(x.dtype)


def mla_prompt(q_nope, q_rope, c, kr, pos, w_uk, w_uv):
    B, T, H, _ = q_nope.shape
    nb = T // Q_BLOCK
    k_nope = jnp.einsum('bkr,rhd->bkhd', c, w_uk)
    v = jnp.einsum('bkr,rhd->bkhd', c, w_uv)
    chunk_id = pos // CHUNK

    def block(args):
        qn, qr, qc = args
        s = (jnp.einsum('bqhd,bkhd->bhqk', qn, k_nope)
             + jnp.einsum('bqhe,bke->bhqk', qr, kr)).astype(jnp.float32) * MLA_SCALE
        mask = chunk_id[None, :] <= qc[:, None]
        p = jax.nn.softmax(jnp.where(mask[None, None], s, -jnp.inf), axis=-1).astype(v.dtype)
        return jnp.einsum('bhqk,bkhd->bqhd', p, v)

    def to_blocks(a):
        return jnp.moveaxis(a.reshape((B, nb, Q_BLOCK) + a.shape[2:]), 1, 0)

    out = lax.map(block, (to_blocks(q_nope), to_blocks(q_rope), chunk_id.reshape(nb, Q_BLOCK)))
    return jnp.moveaxis(out, 0, 1).reshape(B, T, H, V_HEAD)


def mla_sample(q_nope, q_rope, c_all, kr_all, w_uk, w_uv):
    q_lat = jnp.einsum('bqhd,rhd->bqhr', q_nope, w_uk)
    s = (jnp.einsum('bqhr,bkr->bhqk', q_lat, c_all)
         + jnp.einsum('bqhe,bke->bhqk', q_rope, kr_all)).astype(jnp.float32) * MLA_SCALE
    p = jax.nn.softmax(s, axis=-1).astype(c_all.dtype)
    o_lat = jnp.einsum('bhqk,bkr->bqhr', p, c_all)
    return jnp.einsum('bqhr,rhd->bqhd', o_lat, w_uv)


def retention(q, k, v, s0):
    B, T, H, _ = q.shape
    cs = min(CHUNK, T)
    n = T // cs
    lg = jnp.log1p(-jnp.exp2(-5.0 - jnp.arange(H, dtype=jnp.float32)))
    j = jnp.arange(cs, dtype=jnp.float32)
    diff = j[:, None] - j[None, :]
    dmask = jnp.where(diff[None] >= 0, jnp.exp(jnp.maximum(diff, 0.0)[None] * lg[:, None, None]), 0.0)
    q_dec = jnp.exp((j + 1.0)[None] * lg[:, None])[..., None]
    k_dec = jnp.exp((cs - 1.0 - j)[None] * lg[:, None])[..., None]
    c_dec = jnp.exp(cs * lg)[:, None, None]

    def to_chunks(a):
        a = a.astype(jnp.float32)
        return a.reshape(B, n, cs, H, a.shape[-1]).transpose(1, 0, 3, 2, 4)

    def step(S, inp):
        qc, kc, vc = inp
        att = jnp.einsum('bhid,bhjd->bhij', qc, kc) * dmask
        o = jnp.einsum('bhij,bhje->bhie', att, vc) + jnp.einsum('bhid,bhde->bhie', qc * q_dec, S)
        S = S * c_dec + jnp.einsum('bhjd,bhje->bhde', kc * k_dec, vc)
        return S, o

    S, o = lax.scan(step, s0.astype(jnp.float32), (to_chunks(q), to_chunks(k), to_chunks(v)))
    o = o.transpose(1, 0, 3, 2, 4).reshape(B, T, H, v.shape[-1])
    return o, S


def trunk_layer(x, pos, past_c, past_kr, ret_s0, ffn1_norm, ffn1_w13, ffn1_w2, mix_norm, w_in,
                q_norm, kv_norm, w_uq, w_uk, w_uv, w_mla_out, w_ret_out, w_out,
                ffn2_norm, ffn2_w13, ffn2_w2):
    B, T, _ = x.shape
    x = x + 0.5 * swiglu(rms_norm(x, ffn1_norm), ffn1_w13, ffn1_w2)
    u = rms_norm(x, mix_norm)
    q_lat, c_kv, k_r, r_q, r_k, r_v, r_g, g_mla, g_ret = jnp.split(u @ w_in, _in_offsets(), axis=-1)
    q = (rms_norm(q_lat, q_norm) @ w_uq).reshape(B, T, MLA_HEADS, MLA_QK)
    q_nope = q[..., :QK_NOPE]
    q_rope = rope(q[..., QK_NOPE:], pos)
    c_kv = rms_norm(c_kv, kv_norm)
    k_r = rope(k_r, pos)
    if past_c is None:
        a = mla_prompt(q_nope, q_rope, c_kv, k_r, pos, w_uk, w_uv)
    else:
        a = mla_sample(q_nope, q_rope, jnp.concatenate([past_c, c_kv], axis=1),
                       jnp.concatenate([past_kr, k_r], axis=1), w_uk, w_uv)
    a = a.reshape(B, T, MLA_HEADS * V_HEAD) @ w_mla_out
    r_q = rope(r_q.reshape(B, T, RET_HEADS, RET_DK), pos)
    r_k = rope(r_k.reshape(B, T, RET_HEADS, RET_DK), pos) * RET_K_SCALE
    r_v = r_v.reshape(B, T, RET_HEADS, RET_DV)
    if ret_s0 is None:
        ret_s0 = jnp.zeros((B, RET_HEADS, RET_DK, RET_DV), jnp.float32)
    r, s_new = retention(r_q, r_k, r_v, ret_s0)
    r = head_group_norm(r).reshape(B, T, RET_V_W).astype(x.dtype)
    r = (jax.nn.silu(r_g) * r) @ w_ret_out
    x = x + (jax.nn.sigmoid(g_mla) * a + jax.nn.sigmoid(g_ret) * r) @ w_out
    x = x + 0.5 * swiglu(rms_norm(x, ffn2_norm), ffn2_w13, ffn2_w2)
    return x, c_kv, k_r, s_new.astype(x.dtype)


def setup_inputs(seed: int = 0) -> dict:
    key = jax.random.key(seed)
    ks = jax.random.split(key, 24)
    f32 = jnp.float32

    def w(k, shape, fan_in):
        return jax.random.normal(k, shape, f32) * (fan_in ** -0.5)

    def gain(k, shape):
        return 1.0 + 0.01 * jax.random.normal(k, shape, f32)

    return {
        "x_prompt": jax.random.normal(ks[0], (BATCH, SEQ, D_MODEL), f32),
        "x_sample": jax.random.normal(ks[1], (DEC_BATCH, DEC_SEQ, D_MODEL), f32),
        "cache_ckv": jax.random.normal(ks[2], (DEPTH, DEC_BATCH, PAST_LEN, KV_LORA), f32),
        "cache_krope": jax.random.normal(ks[3], (DEPTH, DEC_BATCH, PAST_LEN, QK_ROPE), f32),
        "state_ret": jax.random.normal(ks[4], (DEPTH, DEC_BATCH, RET_HEADS, RET_DK, RET_DV), f32),
        "ffn1_norm": gain(ks[5], (DEPTH, D_MODEL)),
        "ffn1_w13": w(ks[6], (DEPTH, D_MODEL, 2 * D_FF), D_MODEL),
        "ffn1_w2": w(ks[7], (DEPTH, D_FF, D_MODEL), D_FF),
        "mix_norm": gain(ks[8], (DEPTH, D_MODEL)),
        "w_in": w(ks[9], (DEPTH, D_MODEL, IN_WIDTH), D_MODEL),
        "q_norm": gain(ks[10], (DEPTH, Q_LORA)),
        "kv_norm": gain(ks[11], (DEPTH, KV_LORA)),
        "w_uq": w(ks[12], (DEPTH, Q_LORA, MLA_HEADS * MLA_QK), Q_LORA),
        "w_uk": w(ks[13], (DEPTH, KV_LORA, MLA_HEADS, QK_NOPE), KV_LORA),
        "w_uv": w(ks[14], (DEPTH, KV_LORA, MLA_HEADS, V_HEAD), KV_LORA),
        "w_mla_out": w(ks[15], (DEPTH, MLA_HEADS * V_HEAD, D_MODEL), MLA_HEADS * V_HEAD),
        "w_ret_out": w(ks[16], (DEPTH, RET_V_W, D_MODEL), RET_V_W),
        "w_out": w(ks[17], (DEPTH, D_MODEL, D_MODEL), D_MODEL),
        "ffn2_norm": gain(ks[18], (DEPTH, D_MODEL)),
        "ffn2_w13": w(ks[19], (DEPTH, D_MODEL, 2 * D_FF), D_MODEL),
        "ffn2_w2": w(ks[20], (DEPTH, D_FF, D_MODEL), D_FF),
        "final_norm": gain(ks[21], (D_MODEL,)),
    }


def reference(x_prompt, x_sample, cache_ckv, cache_krope, state_ret, ffn1_norm, ffn1_w13, ffn1_w2,
              mix_norm, w_in, q_norm, kv_norm, w_uq, w_uk, w_uv, w_mla_out, w_ret_out, w_out,
              ffn2_norm, ffn2_w13, ffn2_w2, final_norm):
    past = cache_ckv.shape[2]
    pos_p = jnp.arange(x_prompt.shape[1])
    pos_s = past + jnp.arange(x_sample.shape[1])
    hp, hs = x_prompt, x_sample
    ckv_p, kr_p, ret_p, ckv_s, kr_s, ret_s = [], [], [], [], [], []
    for l in range(DEPTH):
        lw = (ffn1_norm[l], ffn1_w13[l], ffn1_w2[l], mix_norm[l], w_in[l], q_norm[l], kv_norm[l],
              w_uq[l], w_uk[l], w_uv[l], w_mla_out[l], w_ret_out[l], w_out[l],
              ffn2_norm[l], ffn2_w13[l], ffn2_w2[l])
        hp, c1, k1, s1 = trunk_layer(hp, pos_p, None, None, None, *lw)
        hs, c2, k2, s2 = trunk_layer(hs, pos_s, cache_ckv[l], cache_krope[l], state_ret[l], *lw)
        ckv_p.append(c1); kr_p.append(k1); ret_p.append(s1)
        ckv_s.append(c2); kr_s.append(k2); ret_s.append(s2)
    y_prompt = rms_norm(hp, final_norm)
    y_sample = rms_norm(hs, final_norm)
    return (y_prompt, y_sample, jnp.stack(ckv_p), jnp.stack(kr_p), jnp.stack(ret_p),
            jnp.stack(ckv_s), jnp.stack(kr_s), jnp.stack(ret_s))
```

```python
import functools

import jax
import jax.numpy as jnp
from jax import lax
from jax.experimental import pallas as pl
from jax.experimental.pallas import tpu as pltpu

D_MODEL = 2048
DEPTH = 2
CHUNK = 64
MLA_HEADS = 8
QK_NOPE = 128
QK_ROPE = 64
V_HEAD = 128
Q_LORA = 512
KV_LORA = 512
MLA_QK = QK_NOPE + QK_ROPE
MLA_SCALE = MLA_QK ** -0.5
RET_HEADS = 8
RET_DK = 128
RET_DV = 256
RET_QK_W = RET_HEADS * RET_DK
RET_V_W = RET_HEADS * RET_DV
RET_K_SCALE = RET_DK ** -0.5
D_FF = 5632
ROPE_THETA = 10000.0
NORM_EPS = 1e-6
GN_EPS = 1e-5

MLA_IN_W = Q_LORA + KV_LORA + QK_ROPE
REST_W = 2 * RET_QK_W + 2 * RET_V_W + 2 * D_MODEL

LANE = 128
VMEM_LIMIT = 56 * 1024 * 1024

BF16 = jnp.bfloat16
F32 = jnp.float32
_NT = (((1,), (1,)), ((), ()))
_TN = (((0,), (0,)), ((), ()))


def _params(sem):
    return pltpu.CompilerParams(dimension_semantics=sem, vmem_limit_bytes=VMEM_LIMIT)


def _rms(x, g):
    return x * lax.rsqrt(jnp.mean(x * x, axis=-1, keepdims=True) + NORM_EPS) * g


def _resident(shape):
    nd = len(shape)
    return pl.BlockSpec(shape, lambda *_: (0,) * nd, pipeline_mode=pl.Buffered(1))


def _ffn_kernel(x_ref, g_ref, w1_ref, w3_ref, w2_ref, o_ref, xn_ref):
    j = pl.program_id(1)

    @pl.when(j == 0)
    def _():
        xn_ref[...] = _rms(x_ref[...], g_ref[...]).astype(BF16)
        o_ref[...] = jnp.zeros_like(o_ref)

    xn = xn_ref[...]
    a = jnp.dot(xn, w1_ref[...], preferred_element_type=F32)
    b = jnp.dot(xn, w3_ref[...], preferred_element_type=F32)
    h = (a * jax.nn.sigmoid(a) * b).astype(BF16)
    o_ref[...] += jnp.dot(h, w2_ref[...], preferred_element_type=F32)

    @pl.when(j == pl.num_programs(1) - 1)
    def _():
        o_ref[...] = x_ref[...] + 0.5 * o_ref[...]


def _ffn(x, g, w13, w2, *, tm, tf):
    m = x.shape[0]
    nf = D_FF // tf
    return pl.pallas_call(
        _ffn_kernel,
        grid=(m // tm, nf),
        in_specs=[
            pl.BlockSpec((tm, D_MODEL), lambda i, j: (i, 0)),
            pl.BlockSpec((1, D_MODEL), lambda i, j: (0, 0)),
            pl.BlockSpec((D_MODEL, tf), lambda i, j: (0, j)),
            pl.BlockSpec((D_MODEL, tf), lambda i, j: (0, j + nf)),
            pl.BlockSpec((tf, D_MODEL), lambda i, j: (j, 0)),
        ],
        out_specs=pl.BlockSpec((tm, D_MODEL), lambda i, j: (i, 0)),
        out_shape=jax.ShapeDtypeStruct((m, D_MODEL), F32),
        scratch_shapes=[pltpu.VMEM((tm, D_MODEL), BF16)],
        compiler_params=_params(("parallel", "arbitrary")),
        name="ffn",
    )(x, g, w13, w13, w2)


def _mla_proj_kernel(x_ref, g_ref, wa_ref, qg_ref, kvg_ref, wuq_ref, wukv_ref, krt_ref, cq_ref, sq_ref,
                     q_out, ckv_out, kr_out, *kv_outs):
    u = _rms(x_ref[...], g_ref[...]).astype(BF16)
    t = jnp.dot(u, wa_ref[...], preferred_element_type=F32)
    qn = _rms(t[:, :Q_LORA], qg_ref[...]).astype(BF16)
    c = _rms(t[:, Q_LORA:Q_LORA + KV_LORA], kvg_ref[...])
    ckv_out[...] = c
    z = t[:, Q_LORA + KV_LORA:] * krt_ref[...]
    kr = (z + pltpu.roll(z, QK_ROPE, 1))[:, :QK_ROPE]
    kr_out[...] = kr
    tq = jnp.dot(qn, wuq_ref[...], preferred_element_type=F32)
    nope_w = MLA_HEADS * QK_NOPE
    rope_w = MLA_HEADS * QK_ROPE
    cq = cq_ref[...]
    sq = sq_ref[...]
    for pair in range(MLA_HEADS // 2):
        lo = nope_w + pair * LANE
        qr = (tq[:, lo:lo + LANE] * cq + tq[:, lo + rope_w:lo + rope_w + LANE] * sq) * MLA_SCALE
        for e in range(2):
            h = 2 * pair + e
            q_out[h, :, :QK_NOPE] = (tq[:, h * QK_NOPE:(h + 1) * QK_NOPE] * MLA_SCALE).astype(BF16)
            q_out[h, :, QK_NOPE:] = qr[:, e * QK_ROPE:(e + 1) * QK_ROPE].astype(BF16)
    if kv_outs:
        k_out, v_out = kv_outs
        kv = jnp.dot(c.astype(BF16), wukv_ref[...], preferred_element_type=F32)
        krb = kr.astype(BF16)
        for h in range(MLA_HEADS):
            k_out[h, :, :QK_NOPE] = kv[:, h * QK_NOPE:(h + 1) * QK_NOPE].astype(BF16)
            k_out[h, :, QK_NOPE:] = krb
            v_out[h] = kv[:, nope_w + h * V_HEAD:nope_w + (h + 1) * V_HEAD].astype(BF16)


def _mla_proj(x, g, wa, qg, kvg, wuq, wukv, krt, cq, sq, *, tm, emit_kv):
    m = x.shape[0]
    nt = krt.shape[0] // tm
    row = lambda i: (i, 0)
    tab = lambda i: (i % nt, 0)
    head = lambda i: (0, i, 0)
    out_specs = [
        pl.BlockSpec((MLA_HEADS, tm, MLA_QK), head),
        pl.BlockSpec((tm, KV_LORA), row),
        pl.BlockSpec((tm, QK_ROPE), row),
    ]
    out_shape = [
        jax.ShapeDtypeStruct((MLA_HEADS, m, MLA_QK), BF16),
        jax.ShapeDtypeStruct((m, KV_LORA), F32),
        jax.ShapeDtypeStruct((m, QK_ROPE), F32),
    ]
    if emit_kv:
        out_specs += [pl.BlockSpec((MLA_HEADS, tm, MLA_QK), head), pl.BlockSpec((MLA_HEADS, tm, V_HEAD), head)]
        out_shape += [jax.ShapeDtypeStruct((MLA_HEADS, m, MLA_QK), BF16),
                      jax.ShapeDtypeStruct((MLA_HEADS, m, V_HEAD), BF16)]
    return pl.pallas_call(
        _mla_proj_kernel,
        grid=(m // tm,),
        in_specs=[
            pl.BlockSpec((tm, D_MODEL), row),
            _resident((1, D_MODEL)),
            _resident(wa.shape),
            _resident((1, Q_LORA)),
            _resident((1, KV_LORA)),
            _resident(wuq.shape),
            _resident(wukv.shape),
            pl.BlockSpec((tm, LANE), tab),
            pl.BlockSpec((tm, LANE), tab),
            pl.BlockSpec((tm, LANE), tab),
        ],
        out_specs=out_specs,
        out_shape=out_shape,
        compiler_params=_params(("parallel",)),
        name="mla_proj",
    )(x, g, wa, qg, kvg, wuq, wukv, krt, cq, sq)


PROJ_TN = 1024


def _proj_kernel(x_ref, g_ref, w_ref, c_ref, s_ref, o_ref, u_ref):
    j = pl.program_id(1)

    @pl.when(j == 0)
    def _():
        u_ref[...] = _rms(x_ref[...], g_ref[...]).astype(BF16)

    acc = jnp.dot(u_ref[...], w_ref[...], preferred_element_type=F32)

    @pl.when(j < 2)
    def _():
        scale = jnp.where(j == 1, RET_K_SCALE, 1.0).astype(F32)
        c = c_ref[...] * scale
        s = s_ref[...] * scale
        for h in range(PROJ_TN // RET_DK):
            xh = acc[:, h * RET_DK:(h + 1) * RET_DK]
            o_ref[:, h * RET_DK:(h + 1) * RET_DK] = (xh * c + pltpu.roll(xh, RET_DK // 2, 1) * s).astype(BF16)

    @pl.when((j >= 2) & (j < 4))
    def _():
        o_ref[...] = acc.astype(BF16)

    @pl.when((j >= 4) & (j < 6))
    def _():
        o_ref[...] = (acc * jax.nn.sigmoid(acc)).astype(BF16)

    @pl.when(j >= 6)
    def _():
        o_ref[...] = jax.nn.sigmoid(acc).astype(BF16)


def _proj(x, g, w, c128, s128, *, tm):
    m = x.shape[0]
    nt = c128.shape[0] // tm
    return pl.pallas_call(
        _proj_kernel,
        grid=(m // tm, REST_W // PROJ_TN),
        in_specs=[
            pl.BlockSpec((tm, D_MODEL), lambda i, j: (i, 0)),
            pl.BlockSpec((1, D_MODEL), lambda i, j: (0, 0)),
            pl.BlockSpec((D_MODEL, PROJ_TN), lambda i, j: (0, j)),
            pl.BlockSpec((tm, LANE), lambda i, j: (i % nt, 0)),
            pl.BlockSpec((tm, LANE), lambda i, j: (i % nt, 0)),
        ],
        out_specs=pl.BlockSpec((tm, PROJ_TN), lambda i, j: (i, j)),
        out_shape=jax.ShapeDtypeStruct((m, REST_W), BF16),
        scratch_shapes=[pltpu.VMEM((tm, D_MODEL), BF16)],
        compiler_params=_params(("parallel", "arbitrary")),
        name="proj",
    )(x, g, w, c128, s128)


def _attn_kernel(q_ref, k_ref, v_ref, o_ref, m_ref, l_ref, acc_ref, *, tq, tk):
    i = pl.program_id(2)
    q = q_ref[...]
    m_ref[...] = jnp.full_like(m_ref, -jnp.inf)
    l_ref[...] = jnp.zeros_like(l_ref)
    acc_ref[...] = jnp.zeros_like(acc_ref)

    def step(j, masked):
        k = k_ref[pl.ds(pl.multiple_of(j * tk, tk), tk), :]
        v = v_ref[pl.ds(pl.multiple_of(j * tk, tk), tk), :]
        s = lax.dot_general(q, k, _NT, preferred_element_type=F32)
        if masked:
            qc = (i * tq + lax.broadcasted_iota(jnp.int32, (tq, tk), 0)) // CHUNK
            kc = (j * tk + lax.broadcasted_iota(jnp.int32, (tq, tk), 1)) // CHUNK
            s = jnp.where(kc <= qc, s, -jnp.inf)
        m_old = m_ref[...]
        m_new = jnp.maximum(m_old, jnp.max(s, axis=-1, keepdims=True))
        alpha = jnp.exp(m_old - m_new)
        p = jnp.exp(s - m_new)
        l_ref[...] = alpha * l_ref[...] + jnp.sum(p, axis=-1, keepdims=True)
        acc_ref[...] = alpha * acc_ref[...] + jnp.dot(p.astype(BF16), v, preferred_element_type=F32)
        m_ref[...] = m_new

    n_full = i * (tq // tk)

    def body(j, carry):
        step(j, False)
        return carry

    lax.fori_loop(0, n_full, body, 0)
    for d in range(tq // tk):
        step(n_full + d, True)
    o_ref[...] = (acc_ref[...] / l_ref[...]).astype(BF16)


def _attn(q, k, v, *, batch, seq, tq, tk):
    nq = seq // tq
    return pl.pallas_call(
        functools.partial(_attn_kernel, tq=tq, tk=tk),
        grid=(batch, MLA_HEADS, nq),
        in_specs=[
            pl.BlockSpec((None, tq, MLA_QK), lambda b, h, i: (h, b * nq + i, 0)),
            pl.BlockSpec((None, seq, MLA_QK), lambda b, h, i: (h, b, 0)),
            pl.BlockSpec((None, seq, V_HEAD), lambda b, h, i: (h, b, 0)),
        ],
        out_specs=pl.BlockSpec((tq, V_HEAD), lambda b, h, i: (b * nq + i, h)),
        out_shape=jax.ShapeDtypeStruct((batch * seq, MLA_HEADS * V_HEAD), BF16),
        scratch_shapes=[pltpu.VMEM((tq, 1), F32), pltpu.VMEM((tq, 1), F32), pltpu.VMEM((tq, V_HEAD), F32)],
        compiler_params=_params(("parallel", "parallel", "arbitrary")),
        name="attn_prompt",
    )(q, k, v)


def _attn_sample_kernel(q_ref, cn_ref, krn_ref, cc_ref, krc_ref, wukt_ref, wuv_ref, o_ref, ql_ref, qr_ref,
                        *, tq):
    for h in range(MLA_HEADS):
        qh = q_ref[h]
        ql_ref[h * tq:(h + 1) * tq, :] = jnp.dot(
            qh[:, :QK_NOPE], wukt_ref[h], preferred_element_type=F32).astype(BF16)
        qr_ref[h * tq:(h + 1) * tq, :] = qh[:, QK_NOPE:]
    ql = ql_ref[...]
    qr = qr_ref[...]
    cc = cc_ref[...].astype(BF16)
    krc = krc_ref[...].astype(BF16)
    cn = cn_ref[...].astype(BF16)
    krn = krn_ref[...].astype(BF16)
    s_c = (lax.dot_general(ql, cc, _NT, preferred_element_type=F32)
           + lax.dot_general(qr, krc, _NT, preferred_element_type=F32))
    s_n = (lax.dot_general(ql, cn, _NT, preferred_element_type=F32)
           + lax.dot_general(qr, krn, _NT, preferred_element_type=F32))
    m = jnp.maximum(jnp.max(s_c, axis=-1, keepdims=True), jnp.max(s_n, axis=-1, keepdims=True))
    p_c = jnp.exp(s_c - m)
    p_n = jnp.exp(s_n - m)
    l = jnp.sum(p_c, axis=-1, keepdims=True) + jnp.sum(p_n, axis=-1, keepdims=True)
    ol = (jnp.dot(p_c.astype(BF16), cc, preferred_element_type=F32)
          + jnp.dot(p_n.astype(BF16), cn, preferred_element_type=F32)) / l
    ol = ol.astype(BF16)
    for h in range(MLA_HEADS):
        o_ref[:, h * V_HEAD:(h + 1) * V_HEAD] = jnp.dot(
            ol[h * tq:(h + 1) * tq], wuv_ref[h], preferred_element_type=F32).astype(BF16)


def _attn_sample(q, c_new, kr_new, cache_c, cache_kr, layer, wukt, wuv, *, batch, tq):
    past = cache_c.shape[2]
    return pl.pallas_call(
        functools.partial(_attn_sample_kernel, tq=tq),
        grid=(batch,),
        in_specs=[
            pl.BlockSpec((MLA_HEADS, tq, MLA_QK), lambda b: (0, b, 0)),
            pl.BlockSpec((tq, KV_LORA), lambda b: (b, 0)),
            pl.BlockSpec((tq, QK_ROPE), lambda b: (b, 0)),
            pl.BlockSpec((None, None, past, KV_LORA), lambda b: (layer, b, 0, 0)),
            pl.BlockSpec((None, None, past, QK_ROPE), lambda b: (layer, b, 0, 0)),
            _resident(wukt.shape),
            _resident(wuv.shape),
        ],
        out_specs=pl.BlockSpec((tq, MLA_HEADS * V_HEAD), lambda b: (b, 0)),
        out_shape=jax.ShapeDtypeStruct((batch * tq, MLA_HEADS * V_HEAD), BF16),
        scratch_shapes=[pltpu.VMEM((MLA_HEADS * tq, KV_LORA), BF16), pltpu.VMEM((MLA_HEADS * tq, QK_ROPE), BF16)],
        compiler_params=_params(("parallel",)),
        name="attn_sample",
    )(q, c_new, kr_new, cache_c, cache_kr, wukt, wuv)


def _ret_kernel(*refs, seq, cs, has_state):
    if has_state:
        q_ref, k_ref, v_ref, dm_ref, qd_ref, kd_ref, cd_ref, s0_ref, o_ref, s_out, s_ref = refs
        s_ref[...] = s0_ref[...]
    else:
        q_ref, k_ref, v_ref, dm_ref, qd_ref, kd_ref, cd_ref, o_ref, s_out, s_ref = refs
        s_ref[...] = jnp.zeros_like(s_ref)
    dm = dm_ref[...]
    qd = qd_ref[...]
    kd = kd_ref[...]
    cd = cd_ref[...]

    def body(c, carry):
        rows = pl.ds(pl.multiple_of(c * cs, cs), cs)
        qc = q_ref[rows, :]
        kc = k_ref[rows, :]
        vc = v_ref[rows, :]
        state = s_ref[...]
        att = lax.dot_general(qc, kc, _NT, preferred_element_type=F32) * dm
        o = (jnp.dot(att.astype(BF16), vc, preferred_element_type=F32)
             + jnp.dot(qc, state.astype(BF16), preferred_element_type=F32) * qd)
        kdec = (kc.astype(F32) * kd).astype(BF16)
        s_ref[...] = state * cd + lax.dot_general(kdec, vc, _TN, preferred_element_type=F32)
        mu = jnp.mean(o, axis=-1, keepdims=True)
        d = o - mu
        var = jnp.mean(d * d, axis=-1, keepdims=True)
        o_ref[rows, :] = (d * lax.rsqrt(var + GN_EPS)).astype(BF16)
        return carry

    lax.fori_loop(0, seq // cs, body, 0)
    s_out[...] = s_ref[...]


def _ret_tables(cs):
    lg = jnp.log1p(-jnp.exp2(-5.0 - jnp.arange(RET_HEADS, dtype=F32)))
    j = jnp.arange(cs, dtype=F32)
    diff = j[:, None] - j[None, :]
    dmask = jnp.where(diff[None] >= 0, jnp.exp(jnp.maximum(diff, 0.0)[None] * lg[:, None, None]), 0.0)
    q_dec = jnp.exp((j + 1.0)[None] * lg[:, None])[..., None]
    k_dec = jnp.exp((cs - 1.0 - j)[None] * lg[:, None])[..., None]
    c_dec = jnp.exp(cs * lg)[:, None, None]
    return (dmask,
            jnp.broadcast_to(q_dec, (RET_HEADS, cs, RET_DV)),
            jnp.broadcast_to(k_dec, (RET_HEADS, cs, RET_DK)),
            jnp.broadcast_to(c_dec, (RET_HEADS, 1, RET_DV)))


def _retention(proj, tables, state, layer, *, batch, seq, cs):
    dmask, q_dec, k_dec, c_dec = tables
    k_col = RET_QK_W // RET_DK
    v_col = 2 * RET_QK_W // RET_DV
    in_specs = [
        pl.BlockSpec((seq, RET_DK), lambda b, h: (b, h)),
        pl.BlockSpec((seq, RET_DK), lambda b, h: (b, k_col + h)),
        pl.BlockSpec((seq, RET_DV), lambda b, h: (b, v_col + h)),
        pl.BlockSpec((None, cs, cs), lambda b, h: (h, 0, 0)),
        pl.BlockSpec((None, cs, RET_DV), lambda b, h: (h, 0, 0)),
        pl.BlockSpec((None, cs, RET_DK), lambda b, h: (h, 0, 0)),
        pl.BlockSpec((None, 1, RET_DV), lambda b, h: (h, 0, 0)),
    ]
    args = [proj, proj, proj, dmask, q_dec, k_dec, c_dec]
    if state is not None:
        in_specs.append(pl.BlockSpec((None, None, None, RET_DK, RET_DV), lambda b, h: (layer, b, h, 0, 0)))
        args.append(state)
    return pl.pallas_call(
        functools.partial(_ret_kernel, seq=seq, cs=cs, has_state=state is not None),
        grid=(batch, RET_HEADS),
        in_specs=in_specs,
        out_specs=[
            pl.BlockSpec((seq, RET_DV), lambda b, h: (b, h)),
            pl.BlockSpec((None, None, RET_DK, RET_DV), lambda b, h: (b, h, 0, 0)),
        ],
        out_shape=[
            jax.ShapeDtypeStruct((batch * seq, RET_V_W), BF16),
            jax.ShapeDtypeStruct((batch, RET_HEADS, RET_DK, RET_DV), F32),
        ],
        scratch_shapes=[pltpu.VMEM((RET_DK, RET_DV), F32)],
        compiler_params=_params(("parallel", "arbitrary")),
        name="retention",
    )(*args)


def _merge_kernel(a_ref, ro_ref, rg_ref, gm_ref, gr_ref, x_ref, wm_ref, wr_ref, wo_ref, o_ref):
    a = jnp.dot(a_ref[...], wm_ref[...], preferred_element_type=F32)
    r = jnp.dot(rg_ref[...] * ro_ref[...], wr_ref[...], preferred_element_type=F32)
    mix = (gm_ref[...].astype(F32) * a + gr_ref[...].astype(F32) * r).astype(BF16)
    o_ref[...] = x_ref[...] + jnp.dot(mix, wo_ref[...], preferred_element_type=F32)


def _merge(a, ro, proj, x, wm, wr, wo, *, tm):
    m = x.shape[0]
    row = lambda i: (i, 0)
    gate_col = 2 * RET_QK_W + RET_V_W
    c0 = gate_col // D_MODEL
    return pl.pallas_call(
        _merge_kernel,
        grid=(m // tm,),
        in_specs=[
            pl.BlockSpec((tm, MLA_HEADS * V_HEAD), row),
            pl.BlockSpec((tm, RET_V_W), row),
            pl.BlockSpec((tm, D_MODEL), lambda i: (i, c0)),
            pl.BlockSpec((tm, D_MODEL), lambda i: (i, c0 + 1)),
            pl.BlockSpec((tm, D_MODEL), lambda i: (i, c0 + 2)),
            pl.BlockSpec((tm, D_MODEL), row),
            _resident(wm.shape),
            _resident(wr.shape),
            _resident(wo.shape),
        ],
        out_specs=pl.BlockSpec((tm, D_MODEL), row),
        out_shape=jax.ShapeDtypeStruct((m, D_MODEL), F32),
        compiler_params=_params(("parallel",)),
        name="merge",
    )(a, ro, proj, proj, proj, x, wm, wr, wo)


def _norm_kernel(x_ref, g_ref, o_ref):
    o_ref[...] = _rms(x_ref[...], g_ref[...])


def _final_norm(x, g, *, tm):
    m = x.shape[0]
    return pl.pallas_call(
        _norm_kernel,
        grid=(m // tm,),
        in_specs=[pl.BlockSpec((tm, D_MODEL), lambda i: (i, 0)), pl.BlockSpec((1, D_MODEL), lambda i: (0, 0))],
        out_specs=pl.BlockSpec((tm, D_MODEL), lambda i: (i, 0)),
        out_shape=jax.ShapeDtypeStruct((m, D_MODEL), F32),
        compiler_params=_params(("parallel",)),
        name="final_norm",
    )(x, g)


def _rope_tables(pos, d):
    inv = ROPE_THETA ** (-jnp.arange(0, d, 2, dtype=F32) / d)
    ang = pos.astype(F32)[:, None] * inv[None, :]
    cos, sin = jnp.cos(ang), jnp.sin(ang)
    return jnp.concatenate([cos, cos], axis=-1), jnp.concatenate([-sin, sin], axis=-1)


def _half_swap(w):
    d = w.shape[-1]
    return jnp.concatenate([w[..., d // 2:], w[..., :d // 2]], axis=-1)


def _layer_weights(l, ffn1_w13, ffn1_w2, w_in, w_uq, w_uk, w_uv, w_mla_out, w_ret_out, w_out, ffn2_w13, ffn2_w2):
    wi = w_in[l]
    w_kr = wi[:, Q_LORA + KV_LORA:MLA_IN_W]
    wa = jnp.concatenate([wi[:, :MLA_IN_W], _half_swap(w_kr)], axis=1).astype(BF16)
    wq = w_uq[l].reshape(Q_LORA, MLA_HEADS, MLA_QK)
    wq_r = wq[:, :, QK_NOPE:]
    wuq = jnp.concatenate([
        wq[:, :, :QK_NOPE].reshape(Q_LORA, -1),
        wq_r.reshape(Q_LORA, -1),
        _half_swap(wq_r).reshape(Q_LORA, -1)], axis=1).astype(BF16)
    wukv = jnp.concatenate([w_uk[l].reshape(KV_LORA, -1), w_uv[l].reshape(KV_LORA, -1)], axis=1).astype(BF16)
    return dict(
        ffn1_w13=ffn1_w13[l].astype(BF16), ffn1_w2=ffn1_w2[l].astype(BF16),
        ffn2_w13=ffn2_w13[l].astype(BF16), ffn2_w2=ffn2_w2[l].astype(BF16),
        wa=wa, wrest=wi[:, MLA_IN_W:].astype(BF16), wuq=wuq, wukv=wukv,
        wukt=jnp.transpose(w_uk[l], (1, 2, 0)).astype(BF16),
        wuv=jnp.transpose(w_uv[l], (1, 0, 2)).astype(BF16),
        w_mla_out=w_mla_out[l].astype(BF16), w_ret_out=w_ret_out[l].astype(BF16), w_out=w_out[l].astype(BF16),
    )


def _tables(pos):
    c64, s64 = _rope_tables(pos, QK_ROPE)
    c128, s128 = _rope_tables(pos, RET_DK)
    return dict(krt=jnp.concatenate([c64, s64], axis=1), cq=jnp.concatenate([c64, c64], axis=1),
                sq=jnp.concatenate([s64, s64], axis=1), c128=c128, s128=s128)


def kernel(x_prompt, x_sample, cache_ckv, cache_krope, state_ret, ffn1_norm, ffn1_w13, ffn1_w2, mix_norm, w_in,
           q_norm, kv_norm, w_uq, w_uk, w_uv, w_mla_out, w_ret_out, w_out, ffn2_norm, ffn2_w13, ffn2_w2,
           final_norm):
    batch, seq, _ = x_prompt.shape
    dbatch, dseq, _ = x_sample.shape
    past = cache_ckv.shape[2]
    mp, ms = batch * seq, dbatch * dseq

    tab_p = _tables(jnp.arange(seq))
    tab_s = _tables(jnp.tile(past + jnp.arange(dseq), dbatch))
    ret_tab_p = _ret_tables(256)
    ret_tab_s = _ret_tables(dseq)

    hp = x_prompt.reshape(mp, D_MODEL)
    hs = x_sample.reshape(ms, D_MODEL)
    outs = {k: [] for k in ("ckv_p", "kr_p", "ret_p", "ckv_s", "kr_s", "ret_s")}
    row = lambda v: v.reshape(1, -1)

    for l in range(DEPTH):
        w = _layer_weights(l, ffn1_w13, ffn1_w2, w_in, w_uq, w_uk, w_uv, w_mla_out, w_ret_out, w_out,
                           ffn2_w13, ffn2_w2)
        g1, gm, gq, gkv, g2 = (row(ffn1_norm[l]), row(mix_norm[l]), row(q_norm[l]), row(kv_norm[l]),
                               row(ffn2_norm[l]))

        hp = _ffn(hp, g1, w["ffn1_w13"], w["ffn1_w2"], tm=512, tf=512)
        q, ckv, kr, k, v = _mla_proj(hp, gm, w["wa"], gq, gkv, w["wuq"], w["wukv"],
                                     tab_p["krt"], tab_p["cq"], tab_p["sq"], tm=256, emit_kv=True)
        proj = _proj(hp, gm, w["wrest"], tab_p["c128"], tab_p["s128"], tm=512)
        a = _attn(q, k, v, batch=batch, seq=seq, tq=512, tk=512)
        ro, s_new = _retention(proj, ret_tab_p, None, l, batch=batch, seq=seq, cs=256)
        hp = _merge(a, ro, proj, hp, w["w_mla_out"], w["w_ret_out"], w["w_out"], tm=256)
        hp = _ffn(hp, g2, w["ffn2_w13"], w["ffn2_w2"], tm=512, tf=512)
        outs["ckv_p"].append(ckv.reshape(batch, seq, KV_LORA))
        outs["kr_p"].append(kr.reshape(batch, seq, QK_ROPE))
        outs["ret_p"].append(s_new)

        hs = _ffn(hs, g1, w["ffn1_w13"], w["ffn1_w2"], tm=ms, tf=512)
        q, ckv, kr = _mla_proj(hs, gm, w["wa"], gq, gkv, w["wuq"], w["wukv"],
                               tab_s["krt"], tab_s["cq"], tab_s["sq"], tm=ms, emit_kv=False)
        proj = _proj(hs, gm, w["wrest"], tab_s["c128"], tab_s["s128"], tm=ms)
        a = _attn_sample(q, ckv, kr, cache_ckv, cache_krope, l, w["wukt"], w["wuv"], batch=dbatch, tq=dseq)
        ro, s_new = _retention(proj, ret_tab_s, state_ret, l, batch=dbatch, seq=dseq, cs=dseq)
        hs = _merge(a, ro, proj, hs, w["w_mla_out"], w["w_ret_out"], w["w_out"], tm=ms)
        hs = _ffn(hs, g2, w["ffn2_w13"], w["ffn2_w2"], tm=ms, tf=512)
        outs["ckv_s"].append(ckv.reshape(dbatch, dseq, KV_LORA))
        outs["kr_s"].append(kr.reshape(dbatch, dseq, QK_ROPE))
        outs["ret_s"].append(s_new)

    gf = row(final_norm)
    y_prompt = _final_norm(hp, gf, tm=512).reshape(batch, seq, D_MODEL)
    y_sample = _final_norm(hs, gf, tm=ms).reshape(dbatch, dseq, D_MODEL)
    return (y_prompt, y_sample, jnp.stack(outs["ckv_p"]), jnp.stack(outs["kr_p"]), jnp.stack(outs["ret_p"]),
            jnp.stack(outs["ckv_s"]), jnp.stack(outs["kr_s"]), jnp.stack(outs["ret_s"]))
```

```python
import functools
import math

import jax
import jax.numpy as jnp
from jax import lax
from jax.experimental import pallas as pl
from jax.experimental.pallas import tpu as pltpu

D_MODEL = 2048
DEPTH = 2
CHUNK = 64
MLA_HEADS = 8
QK_NOPE = 128
QK_ROPE = 64
V_HEAD = 128
Q_LORA = 512
KV_LORA = 512
MLA_QK = QK_NOPE + QK_ROPE
MLA_SCALE = MLA_QK ** -0.5
RET_HEADS = 8
RET_DK = 128
RET_DV = 256
RET_QK_W = RET_HEADS * RET_DK
RET_V_W = RET_HEADS * RET_DV
RET_K_SCALE = RET_DK ** -0.5
D_FF = 5632
ROPE_THETA = 10000.0
NORM_EPS = 1e-6
GN_EPS = 1e-5

MLA_IN_W = Q_LORA + KV_LORA + QK_ROPE
NOPE_W = MLA_HEADS * QK_NOPE
ROPE_W = MLA_HEADS * QK_ROPE
Q_SCALE = MLA_SCALE * math.log2(math.e)

LANE = 128
VMEM_LIMIT = 56 * 1024 * 1024

BF16 = jnp.bfloat16
F32 = jnp.float32
_NT = (((1,), (1,)), ((), ()))
_TN = (((0,), (0,)), ((), ()))


def _params(sem):
    return pltpu.CompilerParams(dimension_semantics=sem, vmem_limit_bytes=VMEM_LIMIT)


def _rms(x, g):
    return x * lax.rsqrt(jnp.mean(x * x, axis=-1, keepdims=True) + NORM_EPS) * g


def _resident(shape):
    nd = len(shape)
    return pl.BlockSpec(shape, lambda *_: (0,) * nd, pipeline_mode=pl.Buffered(1))


def _ffn_kernel(x_ref, g_ref, w1_ref, w3_ref, w2_ref, o_ref, xn_ref):
    j = pl.program_id(1)

    @pl.when(j == 0)
    def _():
        xn_ref[...] = _rms(x_ref[...], g_ref[...]).astype(BF16)
        o_ref[...] = jnp.zeros_like(o_ref)

    xn = xn_ref[...]
    a = jnp.dot(xn, w1_ref[...], preferred_element_type=F32)
    b = jnp.dot(xn, w3_ref[...], preferred_element_type=F32)
    h = (a * jax.nn.sigmoid(a) * b).astype(BF16)
    o_ref[...] += jnp.dot(h, w2_ref[...], preferred_element_type=F32)

    @pl.when(j == pl.num_programs(1) - 1)
    def _():
        o_ref[...] = x_ref[...] + 0.5 * o_ref[...]


def _ffn(x, g, w13, w2, *, tm, tf):
    m = x.shape[0]
    nf = D_FF // tf
    return pl.pallas_call(
        _ffn_kernel,
        grid=(m // tm, nf),
        in_specs=[
            pl.BlockSpec((tm, D_MODEL), lambda i, j: (i, 0)),
            pl.BlockSpec((1, D_MODEL), lambda i, j: (0, 0)),
            pl.BlockSpec((D_MODEL, tf), lambda i, j: (0, j)),
            pl.BlockSpec((D_MODEL, tf), lambda i, j: (0, j + nf)),
            pl.BlockSpec((tf, D_MODEL), lambda i, j: (j, 0)),
        ],
        out_specs=pl.BlockSpec((tm, D_MODEL), lambda i, j: (i, 0)),
        out_shape=jax.ShapeDtypeStruct((m, D_MODEL), F32),
        scratch_shapes=[pltpu.VMEM((tm, D_MODEL), BF16)],
        compiler_params=_params(("parallel", "arbitrary")),
        name="ffn",
    )(x, g, w13, w13, w2)


def _mla_proj_kernel(x_ref, g_ref, wa_ref, qg_ref, kvg_ref, wuq_ref, wuk_ref, wuvt_ref, krt_ref, cq_ref, sq_ref,
                     u_out, q_out, ckv_out, kr_out, *kv_outs, transposed):
    u = _rms(x_ref[...], g_ref[...]).astype(BF16)
    u_out[...] = u
    t = jnp.dot(u, wa_ref[...], preferred_element_type=F32)
    qn = _rms(t[:, :Q_LORA], qg_ref[...])
    c = _rms(t[:, Q_LORA:Q_LORA + KV_LORA], kvg_ref[...])
    ckv_out[...] = c
    z = t[:, Q_LORA + KV_LORA:] * krt_ref[...]
    kr = (z + pltpu.roll(z, QK_ROPE, 1))[:, :QK_ROPE]
    kr_out[...] = kr
    cq = cq_ref[...]
    sq = sq_ref[...]
    if transposed:
        tq = jnp.dot(wuq_ref[...], qn.T.astype(BF16), preferred_element_type=F32)
        for h in range(MLA_HEADS):
            q_out[h, :QK_NOPE, :] = (tq[h * QK_NOPE:(h + 1) * QK_NOPE] * Q_SCALE).astype(BF16)
            lo = NOPE_W + h * QK_ROPE
            qr = tq[lo:lo + QK_ROPE] * cq + tq[lo + ROPE_W:lo + ROPE_W + QK_ROPE] * sq
            q_out[h, QK_NOPE:, :] = (qr * Q_SCALE).astype(BF16)
        k_out, vt_out = kv_outs
        kn = jnp.dot(c.astype(BF16), wuk_ref[...], preferred_element_type=F32)
        vt = jnp.dot(wuvt_ref[...], c.T.astype(BF16), preferred_element_type=F32)
        krb = kr.astype(BF16)
        for h in range(MLA_HEADS):
            k_out[h, :, :QK_NOPE] = kn[:, h * QK_NOPE:(h + 1) * QK_NOPE].astype(BF16)
            k_out[h, :, QK_NOPE:] = krb
            vt_out[h] = vt[h * V_HEAD:(h + 1) * V_HEAD].astype(BF16)
    else:
        tq = jnp.dot(qn.astype(BF16), wuq_ref[...], preferred_element_type=F32)
        for pair in range(MLA_HEADS // 2):
            lo = NOPE_W + pair * LANE
            qr = (tq[:, lo:lo + LANE] * cq + tq[:, lo + ROPE_W:lo + ROPE_W + LANE] * sq) * Q_SCALE
            for e in range(2):
                h = 2 * pair + e
                q_out[h, :, :QK_NOPE] = (tq[:, h * QK_NOPE:(h + 1) * QK_NOPE] * Q_SCALE).astype(BF16)
                q_out[h, :, QK_NOPE:] = qr[:, e * QK_ROPE:(e + 1) * QK_ROPE].astype(BF16)


def _mla_proj(x, g, wa, qg, kvg, wuq, wuk, wuvt, krt, cq, sq, *, tm, transposed):
    m = x.shape[0]
    row = lambda i: (i, 0)
    out_specs = [pl.BlockSpec((tm, D_MODEL), row)]
    out_shape = [jax.ShapeDtypeStruct((m, D_MODEL), BF16)]
    if transposed:
        nt = krt.shape[0] // tm
        q_tab = pl.BlockSpec((QK_ROPE, tm), lambda i: (0, i % nt))
        out_specs.append(pl.BlockSpec((MLA_HEADS, MLA_QK, tm), lambda i: (0, 0, i)))
        out_shape.append(jax.ShapeDtypeStruct((MLA_HEADS, MLA_QK, m), BF16))
    else:
        nt = krt.shape[0] // tm
        q_tab = pl.BlockSpec((tm, LANE), lambda i: (i % nt, 0))
        out_specs.append(pl.BlockSpec((MLA_HEADS, tm, MLA_QK), lambda i: (0, i, 0)))
        out_shape.append(jax.ShapeDtypeStruct((MLA_HEADS, m, MLA_QK), BF16))
    out_specs += [pl.BlockSpec((tm, KV_LORA), row), pl.BlockSpec((tm, QK_ROPE), row)]
    out_shape += [jax.ShapeDtypeStruct((m, KV_LORA), F32), jax.ShapeDtypeStruct((m, QK_ROPE), F32)]
    if transposed:
        out_specs += [pl.BlockSpec((MLA_HEADS, tm, MLA_QK), lambda i: (0, i, 0)),
                      pl.BlockSpec((MLA_HEADS, V_HEAD, tm), lambda i: (0, 0, i))]
        out_shape += [jax.ShapeDtypeStruct((MLA_HEADS, m, MLA_QK), BF16),
                      jax.ShapeDtypeStruct((MLA_HEADS, V_HEAD, m), BF16)]
    return pl.pallas_call(
        functools.partial(_mla_proj_kernel, transposed=transposed),
        grid=(m // tm,),
        in_specs=[
            pl.BlockSpec((tm, D_MODEL), row),
            _resident((1, D_MODEL)),
            _resident(wa.shape),
            _resident((1, Q_LORA)),
            _resident((1, KV_LORA)),
            _resident(wuq.shape),
            _resident(wuk.shape),
            _resident(wuvt.shape),
            pl.BlockSpec((tm, LANE), lambda i: (i % nt, 0)),
            q_tab,
            q_tab,
        ],
        out_specs=out_specs,
        out_shape=out_shape,
        compiler_params=_params(("parallel",)),
        name="mla_proj",
    )(x, g, wa, qg, kvg, wuq, wuk, wuvt, krt, cq, sq)


MM_SUB = 512


def _mm_kernel(*refs, epilogue, tn):
    if epilogue == "rope":
        u_ref, w_ref, c_ref, s_ref, o_ref = refs
        scale = jnp.where(pl.program_id(1) == 1, RET_K_SCALE, 1.0).astype(F32)
        c = c_ref[...] * scale
        s = s_ref[...] * scale
    else:
        u_ref, w_ref, o_ref = refs
    u = u_ref[...]
    for n in range(tn // MM_SUB):
        cols = slice(n * MM_SUB, (n + 1) * MM_SUB)
        acc = jnp.dot(u, w_ref[:, cols], preferred_element_type=F32)
        if epilogue == "rope":
            for h in range(MM_SUB // RET_DK):
                xh = acc[:, h * RET_DK:(h + 1) * RET_DK]
                lo = n * MM_SUB + h * RET_DK
                o_ref[:, lo:lo + RET_DK] = (xh * c + pltpu.roll(xh, RET_DK // 2, 1) * s).astype(BF16)
        elif epilogue == "silu":
            o_ref[:, cols] = (acc * jax.nn.sigmoid(acc)).astype(BF16)
        elif epilogue == "sigmoid":
            o_ref[:, cols] = jax.nn.sigmoid(acc).astype(BF16)
        else:
            o_ref[:, cols] = acc.astype(BF16)


def _mm(u, w, col0, ncols, epilogue, tables=None, *, tm, tn):
    m = u.shape[0]
    c0 = col0 // tn
    in_specs = [
        pl.BlockSpec((tm, D_MODEL), lambda i, j: (i, 0)),
        pl.BlockSpec((D_MODEL, tn), lambda i, j: (0, c0 + j)),
    ]
    args = [u, w]
    if epilogue == "rope":
        nt = tables[0].shape[0] // tm
        in_specs += [pl.BlockSpec((tm, LANE), lambda i, j: (i % nt, 0))] * 2
        args += list(tables)
    return pl.pallas_call(
        functools.partial(_mm_kernel, epilogue=epilogue, tn=tn),
        grid=(m // tm, ncols // tn),
        in_specs=in_specs,
        out_specs=pl.BlockSpec((tm, tn), lambda i, j: (i, j)),
        out_shape=jax.ShapeDtypeStruct((m, ncols), BF16),
        compiler_params=_params(("parallel", "arbitrary")),
        name="mm_" + epilogue,
    )(*args)


def _attn_kernel(qt_ref, k_ref, vt_ref, bias_ref, o_ref, m_ref, l_ref, acc_ref, s_ref, p_ref, a_ref, *, tq, tk):
    i = pl.program_id(2)
    assert tq == 2 * tk
    qt = qt_ref[...]
    m_ref[...] = jnp.full_like(m_ref, -jnp.inf)
    l_ref[...] = jnp.zeros_like(l_ref)
    acc_ref[...] = jnp.zeros_like(acc_ref)
    p_ref[1] = jnp.zeros((tk, tq), BF16)
    a_ref[1] = jnp.ones((1, tq), F32)

    def scores(j, slot):
        start = pl.multiple_of(j * tk, tk)
        s_ref[slot] = jnp.dot(k_ref[pl.ds(start, tk), :], qt, preferred_element_type=F32)

    def softmax(slot, bias):
        s = s_ref[slot]
        if bias is not None:
            s = s + bias
        m = m_ref[...]
        m_new = jnp.maximum(m, jnp.max(s, axis=0, keepdims=True))
        alpha = jnp.exp2(m - m_new)
        p = jnp.exp2(s - m_new)
        l_ref[...] = alpha * l_ref[...] + jnp.sum(p, axis=0, keepdims=True)
        m_ref[...] = m_new
        p_ref[slot] = p.astype(BF16)
        a_ref[slot] = alpha

    def pv(j, slot):
        start = pl.multiple_of(j * tk, tk)
        acc_ref[...] = a_ref[slot] * acc_ref[...] + jnp.dot(
            vt_ref[:, pl.ds(start, tk)], p_ref[slot], preferred_element_type=F32)

    scores(0, 0)

    def body(t, carry):
        scores(2 * t + 1, 1)
        softmax(0, None)
        pv(jnp.maximum(2 * t - 1, 0), 1)
        scores(2 * t + 2, 0)
        softmax(1, None)
        pv(2 * t, 0)
        return carry

    lax.fori_loop(0, i, body, 0)
    scores(2 * i + 1, 1)
    softmax(0, bias_ref[0])
    pv(jnp.maximum(2 * i - 1, 0), 1)
    softmax(1, bias_ref[1])
    pv(2 * i, 0)
    pv(2 * i + 1, 1)
    o_ref[...] = (acc_ref[...] * (1.0 / l_ref[...])).T.astype(BF16)


def _attn_bias(tq, tk):
    key_chunk = jnp.arange(tq)[:, None] // CHUNK
    query_chunk = jnp.arange(tq)[None, :] // CHUNK
    bias = jnp.where(key_chunk <= query_chunk, 0.0, -jnp.inf).astype(F32)
    return bias.reshape(tq // tk, tk, tq)


def _attn(qt, k, vt, *, batch, seq, tq, tk):
    nq = seq // tq
    bias = _attn_bias(tq, tk)
    return pl.pallas_call(
        functools.partial(_attn_kernel, tq=tq, tk=tk),
        grid=(batch, MLA_HEADS, nq),
        in_specs=[
            pl.BlockSpec((None, MLA_QK, tq), lambda b, h, i: (h, 0, b * nq + i)),
            pl.BlockSpec((None, seq, MLA_QK), lambda b, h, i: (h, b, 0)),
            pl.BlockSpec((None, V_HEAD, seq), lambda b, h, i: (h, 0, b)),
            _resident(bias.shape),
        ],
        out_specs=pl.BlockSpec((tq, V_HEAD), lambda b, h, i: (b * nq + i, h)),
        out_shape=jax.ShapeDtypeStruct((batch * seq, MLA_HEADS * V_HEAD), BF16),
        scratch_shapes=[pltpu.VMEM((1, tq), F32), pltpu.VMEM((1, tq), F32), pltpu.VMEM((V_HEAD, tq), F32),
                        pltpu.VMEM((2, tk, tq), F32), pltpu.VMEM((2, tk, tq), BF16),
                        pltpu.VMEM((2, 1, tq), F32)],
        compiler_params=_params(("parallel", "parallel", "arbitrary")),
        name="attn_prompt",
    )(qt, k, vt, bias)


def _attn_sample_kernel(q_ref, cn_ref, krn_ref, cc_ref, krc_ref, wukt_ref, wuv_ref, o_ref, ql_ref, qr_ref,
                        *, tq):
    for h in range(MLA_HEADS):
        qh = q_ref[h]
        ql_ref[h * tq:(h + 1) * tq, :] = jnp.dot(
            qh[:, :QK_NOPE], wukt_ref[h], preferred_element_type=F32).astype(BF16)
        qr_ref[h * tq:(h + 1) * tq, :] = qh[:, QK_NOPE:]
    ql = ql_ref[...]
    qr = qr_ref[...]
    cc = cc_ref[...].astype(BF16)
    krc = krc_ref[...].astype(BF16)
    cn = cn_ref[...].astype(BF16)
    krn = krn_ref[...].astype(BF16)
    s_c = (lax.dot_general(ql, cc, _NT, preferred_element_type=F32)
           + lax.dot_general(qr, krc, _NT, preferred_element_type=F32))
    s_n = (lax.dot_general(ql, cn, _NT, preferred_element_type=F32)
           + lax.dot_general(qr, krn, _NT, preferred_element_type=F32))
    m = jnp.maximum(jnp.max(s_c, axis=-1, keepdims=True), jnp.max(s_n, axis=-1, keepdims=True))
    p_c = jnp.exp2(s_c - m)
    p_n = jnp.exp2(s_n - m)
    l = jnp.sum(p_c, axis=-1, keepdims=True) + jnp.sum(p_n, axis=-1, keepdims=True)
    ol = (jnp.dot(p_c.astype(BF16), cc, preferred_element_type=F32)
          + jnp.dot(p_n.astype(BF16), cn, preferred_element_type=F32)) / l
    ol = ol.astype(BF16)
    for h in range(MLA_HEADS):
        o_ref[:, h * V_HEAD:(h + 1) * V_HEAD] = jnp.dot(
            ol[h * tq:(h + 1) * tq], wuv_ref[h], preferred_element_type=F32).astype(BF16)


def _attn_sample(q, c_new, kr_new, cache_c, cache_kr, layer, wukt, wuv, *, batch, tq):
    past = cache_c.shape[2]
    return pl.pallas_call(
        functools.partial(_attn_sample_kernel, tq=tq),
        grid=(batch,),
        in_specs=[
            pl.BlockSpec((MLA_HEADS, tq, MLA_QK), lambda b: (0, b, 0)),
            pl.BlockSpec((tq, KV_LORA), lambda b: (b, 0)),
            pl.BlockSpec((tq, QK_ROPE), lambda b: (b, 0)),
            pl.BlockSpec((None, None, past, KV_LORA), lambda b: (layer, b, 0, 0)),
            pl.BlockSpec((None, None, past, QK_ROPE), lambda b: (layer, b, 0, 0)),
            _resident(wukt.shape),
            _resident(wuv.shape),
        ],
        out_specs=pl.BlockSpec((tq, MLA_HEADS * V_HEAD), lambda b: (b, 0)),
        out_shape=jax.ShapeDtypeStruct((batch * tq, MLA_HEADS * V_HEAD), BF16),
        scratch_shapes=[pltpu.VMEM((MLA_HEADS * tq, KV_LORA), BF16), pltpu.VMEM((MLA_HEADS * tq, QK_ROPE), BF16)],
        compiler_params=_params(("parallel",)),
        name="attn_sample",
    )(q, c_new, kr_new, cache_c, cache_kr, wukt, wuv)


def _ret_kernel(*refs, seq, cs, hps, has_state):
    if has_state:
        q_ref, k_ref, v_ref, dm_ref, qd_ref, kd_ref, cd_ref, s0_ref, o_ref, s_out = refs
    else:
        q_ref, k_ref, v_ref, dm_ref, qd_ref, kd_ref, cd_ref, o_ref, s_out = refs
    for h in range(hps):
        dm = dm_ref[h]
        qd = qd_ref[h]
        kd = kd_ref[h]
        cd = cd_ref[h]
        kcol = slice(h * RET_DK, (h + 1) * RET_DK)
        vcol = slice(h * RET_DV, (h + 1) * RET_DV)
        state = s0_ref[h] if has_state else jnp.zeros((RET_DK, RET_DV), F32)
        for c in range(seq // cs):
            rows = slice(c * cs, (c + 1) * cs)
            qc = q_ref[rows, kcol]
            kc = k_ref[rows, kcol]
            vc = v_ref[rows, vcol]
            att = lax.dot_general(qc, kc, _NT, preferred_element_type=F32) * dm
            o = (jnp.dot(att.astype(BF16), vc, preferred_element_type=F32)
                 + jnp.dot(qc, state.astype(BF16), preferred_element_type=F32) * qd)
            kdec = (kc.astype(F32) * kd).astype(BF16)
            state = state * cd + lax.dot_general(kdec, vc, _TN, preferred_element_type=F32)
            mu = jnp.mean(o, axis=-1, keepdims=True)
            d = o - mu
            var = jnp.mean(d * d, axis=-1, keepdims=True)
            o_ref[rows, vcol] = (d * lax.rsqrt(var + GN_EPS)).astype(BF16)
        s_out[h] = state


def _ret_tables(cs):
    lg = jnp.log1p(-jnp.exp2(-5.0 - jnp.arange(RET_HEADS, dtype=F32)))
    j = jnp.arange(cs, dtype=F32)
    diff = j[:, None] - j[None, :]
    dmask = jnp.where(diff[None] >= 0, jnp.exp(jnp.maximum(diff, 0.0)[None] * lg[:, None, None]), 0.0)
    q_dec = jnp.exp((j + 1.0)[None] * lg[:, None])[..., None]
    k_dec = jnp.exp((cs - 1.0 - j)[None] * lg[:, None])[..., None]
    c_dec = jnp.exp(cs * lg)[:, None, None]
    return (dmask,
            jnp.broadcast_to(q_dec, (RET_HEADS, cs, RET_DV)),
            jnp.broadcast_to(k_dec, (RET_HEADS, cs, RET_DK)),
            jnp.broadcast_to(c_dec, (RET_HEADS, 1, RET_DV)))


def _retention(qk, v, tables, state, layer, *, batch, seq, cs, hps):
    dmask, q_dec, k_dec, c_dec = tables
    k_col = RET_HEADS // hps
    head = lambda b, h: (h, 0, 0)
    in_specs = [
        pl.BlockSpec((seq, hps * RET_DK), lambda b, h: (b, h)),
        pl.BlockSpec((seq, hps * RET_DK), lambda b, h: (b, k_col + h)),
        pl.BlockSpec((seq, hps * RET_DV), lambda b, h: (b, h)),
        pl.BlockSpec((hps, cs, cs), head),
        pl.BlockSpec((hps, cs, RET_DV), head),
        pl.BlockSpec((hps, cs, RET_DK), head),
        pl.BlockSpec((hps, 1, RET_DV), head),
    ]
    args = [qk, qk, v, dmask, q_dec, k_dec, c_dec]
    if state is not None:
        in_specs.append(pl.BlockSpec((None, None, hps, RET_DK, RET_DV), lambda b, h: (layer, b, h, 0, 0)))
        args.append(state)
    return pl.pallas_call(
        functools.partial(_ret_kernel, seq=seq, cs=cs, hps=hps, has_state=state is not None),
        grid=(batch, RET_HEADS // hps),
        in_specs=in_specs,
        out_specs=[
            pl.BlockSpec((seq, hps * RET_DV), lambda b, h: (b, h)),
            pl.BlockSpec((None, hps, RET_DK, RET_DV), lambda b, h: (b, h, 0, 0)),
        ],
        out_shape=[
            jax.ShapeDtypeStruct((batch * seq, RET_V_W), BF16),
            jax.ShapeDtypeStruct((batch, RET_HEADS, RET_DK, RET_DV), F32),
        ],
        compiler_params=_params(("parallel", "arbitrary")),
        name="retention",
    )(*args)


def _merge_kernel(a_ref, ro_ref, rg_ref, gm_ref, gr_ref, x_ref, wm_ref, wr_ref, wo_ref, o_ref):
    a = jnp.dot(a_ref[...], wm_ref[...], preferred_element_type=F32)
    r = jnp.dot(rg_ref[...] * ro_ref[...], wr_ref[...], preferred_element_type=F32)
    mix = (gm_ref[...].astype(F32) * a + gr_ref[...].astype(F32) * r).astype(BF16)
    o_ref[...] = x_ref[...] + jnp.dot(mix, wo_ref[...], preferred_element_type=F32)


def _merge(a, ro, rg, gates, x, wm, wr, wo, *, tm):
    m = x.shape[0]
    row = lambda i: (i, 0)
    return pl.pallas_call(
        _merge_kernel,
        grid=(m // tm,),
        in_specs=[
            pl.BlockSpec((tm, MLA_HEADS * V_HEAD), row),
            pl.BlockSpec((tm, RET_V_W), row),
            pl.BlockSpec((tm, RET_V_W), row),
            pl.BlockSpec((tm, D_MODEL), lambda i: (i, 0)),
            pl.BlockSpec((tm, D_MODEL), lambda i: (i, 1)),
            pl.BlockSpec((tm, D_MODEL), row),
            _resident(wm.shape),
            _resident(wr.shape),
            _resident(wo.shape),
        ],
        out_specs=pl.BlockSpec((tm, D_MODEL), row),
        out_shape=jax.ShapeDtypeStruct((m, D_MODEL), F32),
        compiler_params=_params(("parallel",)),
        name="merge",
    )(a, ro, rg, gates, gates, x, wm, wr, wo)


def _norm_kernel(x_ref, g_ref, o_ref):
    o_ref[...] = _rms(x_ref[...], g_ref[...])


def _final_norm(x, g, *, tm):
    m = x.shape[0]
    return pl.pallas_call(
        _norm_kernel,
        grid=(m // tm,),
        in_specs=[pl.BlockSpec((tm, D_MODEL), lambda i: (i, 0)), pl.BlockSpec((1, D_MODEL), lambda i: (0, 0))],
        out_specs=pl.BlockSpec((tm, D_MODEL), lambda i: (i, 0)),
        out_shape=jax.ShapeDtypeStruct((m, D_MODEL), F32),
        compiler_params=_params(("parallel",)),
        name="final_norm",
    )(x, g)


def _rope_tables(pos, d):
    inv = ROPE_THETA ** (-jnp.arange(0, d, 2, dtype=F32) / d)
    ang = pos.astype(F32)[:, None] * inv[None, :]
    cos, sin = jnp.cos(ang), jnp.sin(ang)
    return jnp.concatenate([cos, cos], axis=-1), jnp.concatenate([-sin, sin], axis=-1)


def _half_swap(w):
    d = w.shape[-1]
    return jnp.concatenate([w[..., d // 2:], w[..., :d // 2]], axis=-1)


def _layer_weights(l, ffn1_w13, ffn1_w2, w_in, w_uq, w_uk, w_uv, w_mla_out, w_ret_out, w_out, ffn2_w13, ffn2_w2):
    wi = w_in[l]
    w_kr = wi[:, Q_LORA + KV_LORA:MLA_IN_W]
    wa = jnp.concatenate([wi[:, :MLA_IN_W], _half_swap(w_kr)], axis=1).astype(BF16)
    wq = w_uq[l].reshape(Q_LORA, MLA_HEADS, MLA_QK)
    wq_r = wq[:, :, QK_NOPE:]
    wuq = jnp.concatenate([
        wq[:, :, :QK_NOPE].reshape(Q_LORA, -1),
        wq_r.reshape(Q_LORA, -1),
        _half_swap(wq_r).reshape(Q_LORA, -1)], axis=1).astype(BF16)
    wuk = w_uk[l].reshape(KV_LORA, -1).astype(BF16)
    wuv = w_uv[l].reshape(KV_LORA, -1).astype(BF16)
    return dict(
        ffn1_w13=ffn1_w13[l].astype(BF16), ffn1_w2=ffn1_w2[l].astype(BF16),
        ffn2_w13=ffn2_w13[l].astype(BF16), ffn2_w2=ffn2_w2[l].astype(BF16),
        wa=wa, wrest=wi[:, MLA_IN_W:].astype(BF16), wuq=wuq, wuqt=wuq.T, wuk=wuk, wuvt=wuv.T,
        wukt=jnp.transpose(w_uk[l], (1, 2, 0)).astype(BF16),
        wuv=jnp.transpose(w_uv[l], (1, 0, 2)).astype(BF16),
        w_mla_out=w_mla_out[l].astype(BF16), w_ret_out=w_ret_out[l].astype(BF16), w_out=w_out[l].astype(BF16),
    )


def _tables(pos, transposed):
    c64, s64 = _rope_tables(pos, QK_ROPE)
    c128, s128 = _rope_tables(pos, RET_DK)
    tab = dict(krt=jnp.concatenate([c64, s64], axis=1), c128=c128, s128=s128)
    if transposed:
        tab.update(cq=c64.T, sq=s64.T)
    else:
        tab.update(cq=jnp.concatenate([c64, c64], axis=1), sq=jnp.concatenate([s64, s64], axis=1))
    return tab


def _mixer(h, w, g_mix, g_q, g_kv, tab, *, tm_proj, tm_mm, transposed):
    wuq = w["wuqt"] if transposed else w["wuq"]
    outs = _mla_proj(h, g_mix, w["wa"], g_q, g_kv, wuq, w["wuk"], w["wuvt"], tab["krt"], tab["cq"], tab["sq"],
                     tm=tm_proj, transposed=transposed)
    u = outs[0]
    tn = 1024
    wr = w["wrest"]
    qk = _mm(u, wr, 0, 2 * RET_QK_W, "rope", (tab["c128"], tab["s128"]), tm=tm_mm, tn=tn)
    v = _mm(u, wr, 2 * RET_QK_W, RET_V_W, "none", tm=tm_mm, tn=tn)
    rg = _mm(u, wr, 2 * RET_QK_W + RET_V_W, RET_V_W, "silu", tm=tm_mm, tn=tn)
    gates = _mm(u, wr, 2 * RET_QK_W + 2 * RET_V_W, 2 * D_MODEL, "sigmoid", tm=tm_mm, tn=tn)
    return outs[1:], qk, v, rg, gates


def kernel(x_prompt, x_sample, cache_ckv, cache_krope, state_ret, ffn1_norm, ffn1_w13, ffn1_w2, mix_norm, w_in,
           q_norm, kv_norm, w_uq, w_uk, w_uv, w_mla_out, w_ret_out, w_out, ffn2_norm, ffn2_w13, ffn2_w2,
           final_norm):
    batch, seq, _ = x_prompt.shape
    dbatch, dseq, _ = x_sample.shape
    past = cache_ckv.shape[2]
    mp, ms = batch * seq, dbatch * dseq

    tab_p = _tables(jnp.arange(seq), True)
    tab_s = _tables(jnp.tile(past + jnp.arange(dseq), dbatch), False)
    ret_tab_p = _ret_tables(256)
    ret_tab_s = _ret_tables(dseq)

    hp = x_prompt.reshape(mp, D_MODEL)
    hs = x_sample.reshape(ms, D_MODEL)
    outs = {k: [] for k in ("ckv_p", "kr_p", "ret_p", "ckv_s", "kr_s", "ret_s")}
    row = lambda v: v.reshape(1, -1)

    for l in range(DEPTH):
        w = _layer_weights(l, ffn1_w13, ffn1_w2, w_in, w_uq, w_uk, w_uv, w_mla_out, w_ret_out, w_out,
                           ffn2_w13, ffn2_w2)
        g1, gm, gq, gkv, g2 = (row(ffn1_norm[l]), row(mix_norm[l]), row(q_norm[l]), row(kv_norm[l]),
                               row(ffn2_norm[l]))

        hp = _ffn(hp, g1, w["ffn1_w13"], w["ffn1_w2"], tm=512, tf=512)
        (qt, ckv, kr, k, vt), qk, v, rg, gates = _mixer(hp, w, gm, gq, gkv, tab_p,
                                                        tm_proj=256, tm_mm=1024, transposed=True)
        a = _attn(qt, k, vt, batch=batch, seq=seq, tq=512, tk=256)
        ro, s_new = _retention(qk, v, ret_tab_p, None, l, batch=batch, seq=seq, cs=256, hps=1)
        hp = _merge(a, ro, rg, gates, hp, w["w_mla_out"], w["w_ret_out"], w["w_out"], tm=256)
        hp = _ffn(hp, g2, w["ffn2_w13"], w["ffn2_w2"], tm=512, tf=512)
        outs["ckv_p"].append(ckv.reshape(batch, seq, KV_LORA))
        outs["kr_p"].append(kr.reshape(batch, seq, QK_ROPE))
        outs["ret_p"].append(s_new)

        hs = _ffn(hs, g1, w["ffn1_w13"], w["ffn1_w2"], tm=ms, tf=512)
        (q, ckv, kr), qk, v, rg, gates = _mixer(hs, w, gm, gq, gkv, tab_s,
                                                tm_proj=ms, tm_mm=ms, transposed=False)
        a = _attn_sample(q, ckv, kr, cache_ckv, cache_krope, l, w["wukt"], w["wuv"], batch=dbatch, tq=dseq)
        ro, s_new = _retention(qk, v, ret_tab_s, state_ret, l, batch=dbatch, seq=dseq, cs=dseq, hps=RET_HEADS)
        hs = _merge(a, ro, rg, gates, hs, w["w_mla_out"], w["w_ret_out"], w["w_out"], tm=ms)
        hs = _ffn(hs, g2, w["ffn2_w13"], w["ffn2_w2"], tm=ms, tf=512)
        outs["ckv_s"].append(ckv.reshape(dbatch, dseq, KV_LORA))
        outs["kr_s"].append(kr.reshape(dbatch, dseq, QK_ROPE))
        outs["ret_s"].append(s_new)

    gf = row(final_norm)
    y_prompt = _final_norm(hp, gf, tm=512).reshape(batch, seq, D_MODEL)
    y_sample = _final_norm(hs, gf, tm=ms).reshape(dbatch, dseq, D_MODEL)
    return (y_prompt, y_sample, jnp.stack(outs["ckv_p"]), jnp.stack(outs["kr_p"]), jnp.stack(outs["ret_p"]),
            jnp.stack(outs["ckv_s"]), jnp.stack(outs["kr_s"]), jnp.stack(outs["ret_s"]))
```

```python
import functools
import math

import jax
import jax.numpy as jnp
from jax import lax
from jax.experimental import pallas as pl
from jax.experimental.pallas import tpu as pltpu

D_MODEL = 2048
DEPTH = 2
CHUNK = 64
MLA_HEADS = 8
QK_NOPE = 128
QK_ROPE = 64
V_HEAD = 128
Q_LORA = 512
KV_LORA = 512
MLA_QK = QK_NOPE + QK_ROPE
MLA_SCALE = MLA_QK ** -0.5
RET_HEADS = 8
RET_DK = 128
RET_DV = 256
RET_QK_W = RET_HEADS * RET_DK
RET_V_W = RET_HEADS * RET_DV
RET_K_SCALE = RET_DK ** -0.5
D_FF = 5632
ROPE_THETA = 10000.0
NORM_EPS = 1e-6
GN_EPS = 1e-5

MLA_IN_W = Q_LORA + KV_LORA + QK_ROPE
NOPE_W = MLA_HEADS * QK_NOPE
ROPE_W = MLA_HEADS * QK_ROPE
Q_SCALE = MLA_SCALE * math.log2(math.e)

LANE = 128
VMEM_LIMIT = 56 * 1024 * 1024

BF16 = jnp.bfloat16
F32 = jnp.float32
_NT = (((1,), (1,)), ((), ()))
_TN = (((0,), (0,)), ((), ()))


def _params(sem):
    return pltpu.CompilerParams(dimension_semantics=sem, vmem_limit_bytes=VMEM_LIMIT)


def _rms(x, g):
    return x * lax.rsqrt(jnp.mean(x * x, axis=-1, keepdims=True) + NORM_EPS) * g


def _resident(shape):
    nd = len(shape)
    return pl.BlockSpec(shape, lambda *_: (0,) * nd, pipeline_mode=pl.Buffered(1))


def _ffn_kernel(*refs, cast, final):
    n_in = 6 if final else 5
    x_ref, g_ref, w1_ref, w3_ref, w2_ref = refs[:5]
    o_ref = refs[n_in]
    xn_ref = refs[-1]
    j = pl.program_id(1)

    @pl.when(j == 0)
    def _():
        x = x_ref[...]
        xn_ref[...] = _rms(x, g_ref[...]).astype(BF16)
        o_ref[...] = x

    w1, w3, w2 = w1_ref[...], w3_ref[...], w2_ref[...]
    if cast:
        w1o_ref, w3o_ref, w2o_ref = refs[n_in + 1:n_in + 4]
        w1, w3, w2 = w1.astype(BF16), w3.astype(BF16), (0.5 * w2).astype(BF16)
        w1o_ref[...] = w1
        w3o_ref[...] = w3
        w2o_ref[...] = w2
    xn = xn_ref[...]
    a = jnp.dot(xn, w1, preferred_element_type=F32)
    b = jnp.dot(xn, w3, preferred_element_type=F32)
    h = (a * jax.nn.sigmoid(a) * b).astype(BF16)
    o_ref[...] += jnp.dot(h, w2, preferred_element_type=F32)

    if final:
        @pl.when(j == pl.num_programs(1) - 1)
        def _():
            o_ref[...] = _rms(o_ref[...], refs[5][...])


def _ffn(x, g, w1, w3, w2, gf=None, *, tm, tf, cast_layer=None):
    m = x.shape[0]
    nf = D_FF // tf
    final = gf is not None
    cast = cast_layer is not None
    vec = pl.BlockSpec((1, D_MODEL), lambda i, j: (0, 0))
    w_up = lambda off: pl.BlockSpec((D_MODEL, tf), lambda i, j: (0, j + off))
    w_down = pl.BlockSpec((tf, D_MODEL), lambda i, j: (j, 0))
    if cast:
        w_up_in = lambda off: pl.BlockSpec((None, D_MODEL, tf), lambda i, j: (cast_layer, 0, j + off))
        w_in_specs = [w_up_in(0), w_up_in(nf), pl.BlockSpec((None, tf, D_MODEL), lambda i, j: (cast_layer, j, 0))]
    else:
        w_in_specs = [w_up(0), w_up(0), w_down]
    in_specs = [pl.BlockSpec((tm, D_MODEL), lambda i, j: (i, 0)), vec] + w_in_specs
    args = [x, g, w1, w3, w2]
    if final:
        in_specs.append(vec)
        args.append(gf)
    out_specs = [pl.BlockSpec((tm, D_MODEL), lambda i, j: (i, 0))]
    out_shape = [jax.ShapeDtypeStruct((m, D_MODEL), F32)]
    if cast:
        assert m == tm, "weights must stream through exactly once when their copies are emitted"
        out_specs += [w_up(0), w_up(0), w_down]
        out_shape += [jax.ShapeDtypeStruct((D_MODEL, D_FF), BF16), jax.ShapeDtypeStruct((D_MODEL, D_FF), BF16),
                      jax.ShapeDtypeStruct((D_FF, D_MODEL), BF16)]
    outs = pl.pallas_call(
        functools.partial(_ffn_kernel, cast=cast, final=final),
        grid=(m // tm, nf),
        in_specs=in_specs,
        out_specs=out_specs,
        out_shape=out_shape,
        scratch_shapes=[pltpu.VMEM((tm, D_MODEL), BF16)],
        compiler_params=_params(("parallel", "arbitrary")),
        name="ffn_cast" if cast else "ffn",
    )(*args)
    return outs if cast else outs[0]


def _mla_proj_kernel(x_ref, g_ref, wa_ref, qg_ref, kvg_ref, wuq_ref, wuk_ref, wuvt_ref, krt_ref, cq_ref, sq_ref,
                     u_out, q_out, ckv_out, kr_out, *kv_outs, transposed):
    u = _rms(x_ref[...], g_ref[...]).astype(BF16)
    u_out[...] = u
    t = jnp.dot(u, wa_ref[...], preferred_element_type=F32)
    qn = _rms(t[:, :Q_LORA], qg_ref[...])
    c = _rms(t[:, Q_LORA:Q_LORA + KV_LORA], kvg_ref[...])
    ckv_out[...] = c
    z = t[:, Q_LORA + KV_LORA:] * krt_ref[...]
    kr = (z + pltpu.roll(z, QK_ROPE, 1))[:, :QK_ROPE]
    kr_out[...] = kr
    cq = cq_ref[...]
    sq = sq_ref[...]
    if transposed:
        tq = jnp.dot(wuq_ref[...], qn.T.astype(BF16), preferred_element_type=F32)
        for h in range(MLA_HEADS):
            q_out[h, :QK_NOPE, :] = (tq[h * QK_NOPE:(h + 1) * QK_NOPE] * Q_SCALE).astype(BF16)
            lo = NOPE_W + h * QK_ROPE
            qr = tq[lo:lo + QK_ROPE] * cq + tq[lo + ROPE_W:lo + ROPE_W + QK_ROPE] * sq
            q_out[h, QK_NOPE:, :] = (qr * Q_SCALE).astype(BF16)
        k_out, vt_out = kv_outs
        kn = jnp.dot(c.astype(BF16), wuk_ref[...], preferred_element_type=F32)
        vt = jnp.dot(wuvt_ref[...], c.T.astype(BF16), preferred_element_type=F32)
        krb = kr.astype(BF16)
        for h in range(MLA_HEADS):
            k_out[h, :, :QK_NOPE] = kn[:, h * QK_NOPE:(h + 1) * QK_NOPE].astype(BF16)
            k_out[h, :, QK_NOPE:] = krb
            vt_out[h] = vt[h * V_HEAD:(h + 1) * V_HEAD].astype(BF16)
    else:
        tq = jnp.dot(qn.astype(BF16), wuq_ref[...], preferred_element_type=F32)
        for pair in range(MLA_HEADS // 2):
            lo = NOPE_W + pair * LANE
            qr = (tq[:, lo:lo + LANE] * cq + tq[:, lo + ROPE_W:lo + ROPE_W + LANE] * sq) * Q_SCALE
            for e in range(2):
                h = 2 * pair + e
                q_out[h, :, :QK_NOPE] = (tq[:, h * QK_NOPE:(h + 1) * QK_NOPE] * Q_SCALE).astype(BF16)
                q_out[h, :, QK_NOPE:] = qr[:, e * QK_ROPE:(e + 1) * QK_ROPE].astype(BF16)


def _mla_proj(x, g, wa, qg, kvg, wuq, wuk, wuvt, krt, cq, sq, *, tm, transposed):
    m = x.shape[0]
    row = lambda i: (i, 0)
    out_specs = [pl.BlockSpec((tm, D_MODEL), row)]
    out_shape = [jax.ShapeDtypeStruct((m, D_MODEL), BF16)]
    if transposed:
        nt = krt.shape[0] // tm
        q_tab = pl.BlockSpec((QK_ROPE, tm), lambda i: (0, i % nt))
        out_specs.append(pl.BlockSpec((MLA_HEADS, MLA_QK, tm), lambda i: (0, 0, i)))
        out_shape.append(jax.ShapeDtypeStruct((MLA_HEADS, MLA_QK, m), BF16))
    else:
        nt = krt.shape[0] // tm
        q_tab = pl.BlockSpec((tm, LANE), lambda i: (i % nt, 0))
        out_specs.append(pl.BlockSpec((MLA_HEADS, tm, MLA_QK), lambda i: (0, i, 0)))
        out_shape.append(jax.ShapeDtypeStruct((MLA_HEADS, m, MLA_QK), BF16))
    out_specs += [pl.BlockSpec((tm, KV_LORA), row), pl.BlockSpec((tm, QK_ROPE), row)]
    out_shape += [jax.ShapeDtypeStruct((m, KV_LORA), F32), jax.ShapeDtypeStruct((m, QK_ROPE), F32)]
    if transposed:
        out_specs += [pl.BlockSpec((MLA_HEADS, tm, MLA_QK), lambda i: (0, i, 0)),
                      pl.BlockSpec((MLA_HEADS, V_HEAD, tm), lambda i: (0, 0, i))]
        out_shape += [jax.ShapeDtypeStruct((MLA_HEADS, m, MLA_QK), BF16),
                      jax.ShapeDtypeStruct((MLA_HEADS, V_HEAD, m), BF16)]
    return pl.pallas_call(
        functools.partial(_mla_proj_kernel, transposed=transposed),
        grid=(m // tm,),
        in_specs=[
            pl.BlockSpec((tm, D_MODEL), row),
            _resident((1, D_MODEL)),
            _resident(wa.shape),
            _resident((1, Q_LORA)),
            _resident((1, KV_LORA)),
            _resident(wuq.shape),
            _resident(wuk.shape),
            _resident(wuvt.shape),
            pl.BlockSpec((tm, LANE), lambda i: (i % nt, 0)),
            q_tab,
            q_tab,
        ],
        out_specs=out_specs,
        out_shape=out_shape,
        compiler_params=_params(("parallel",)),
        name="mla_proj",
    )(x, g, wa, qg, kvg, wuq, wuk, wuvt, krt, cq, sq)


MM_SUB = 512


def _mm_kernel(*refs, epilogue, tn):
    if epilogue == "rope":
        u_ref, w_ref, c_ref, s_ref, o_ref = refs
        scale = jnp.where(pl.program_id(1) == 1, RET_K_SCALE, 1.0).astype(F32)
        c = c_ref[...] * scale
        s = s_ref[...] * scale
    else:
        u_ref, w_ref, o_ref = refs
    u = u_ref[...]
    for n in range(tn // MM_SUB):
        cols = slice(n * MM_SUB, (n + 1) * MM_SUB)
        acc = jnp.dot(u, w_ref[:, cols], preferred_element_type=F32)
        if epilogue == "rope":
            for h in range(MM_SUB // RET_DK):
                xh = acc[:, h * RET_DK:(h + 1) * RET_DK]
                lo = n * MM_SUB + h * RET_DK
                o_ref[:, lo:lo + RET_DK] = (xh * c + pltpu.roll(xh, RET_DK // 2, 1) * s).astype(BF16)
        elif epilogue == "silu":
            o_ref[:, cols] = (acc * jax.nn.sigmoid(acc)).astype(BF16)
        elif epilogue == "sigmoid":
            o_ref[:, cols] = jax.nn.sigmoid(acc).astype(BF16)
        else:
            o_ref[:, cols] = acc.astype(BF16)


def _mm(u, w, col0, ncols, epilogue, tables=None, *, tm, tn):
    m = u.shape[0]
    c0 = col0 // tn
    in_specs = [
        pl.BlockSpec((tm, D_MODEL), lambda i, j: (i, 0)),
        pl.BlockSpec((D_MODEL, tn), lambda i, j: (0, c0 + j)),
    ]
    args = [u, w]
    if epilogue == "rope":
        nt = tables[0].shape[0] // tm
        in_specs += [pl.BlockSpec((tm, LANE), lambda i, j: (i % nt, 0))] * 2
        args += list(tables)
    return pl.pallas_call(
        functools.partial(_mm_kernel, epilogue=epilogue, tn=tn),
        grid=(m // tm, ncols // tn),
        in_specs=in_specs,
        out_specs=pl.BlockSpec((tm, tn), lambda i, j: (i, j)),
        out_shape=jax.ShapeDtypeStruct((m, ncols), BF16),
        compiler_params=_params(("parallel", "arbitrary")),
        name="mm_" + epilogue,
    )(*args)


def _attn_kernel(qt_ref, k_ref, vt_ref, bias_ref, o_ref, m_ref, l_ref, acc_ref, s_ref, p_ref, a_ref, *, tq, tk):
    i = pl.program_id(2)
    assert tq == 2 * tk
    qt = qt_ref[...]
    m_ref[...] = jnp.full_like(m_ref, -jnp.inf)
    l_ref[...] = jnp.zeros_like(l_ref)
    acc_ref[...] = jnp.zeros_like(acc_ref)
    p_ref[1] = jnp.zeros((tk, tq), BF16)
    a_ref[1] = jnp.ones((1, tq), F32)

    def scores(j, slot):
        start = pl.multiple_of(j * tk, tk)
        s_ref[slot] = jnp.dot(k_ref[pl.ds(start, tk), :], qt, preferred_element_type=F32)

    def softmax(slot, bias):
        s = s_ref[slot]
        if bias is not None:
            s = s + bias
        m = m_ref[...]
        m_new = jnp.maximum(m, jnp.max(s, axis=0, keepdims=True))
        alpha = jnp.exp2(m - m_new)
        p = jnp.exp2(s - m_new)
        l_ref[...] = alpha * l_ref[...] + jnp.sum(p, axis=0, keepdims=True)
        m_ref[...] = m_new
        p_ref[slot] = p.astype(BF16)
        a_ref[slot] = alpha

    def pv(j, slot):
        start = pl.multiple_of(j * tk, tk)
        acc_ref[...] = a_ref[slot] * acc_ref[...] + jnp.dot(
            vt_ref[:, pl.ds(start, tk)], p_ref[slot], preferred_element_type=F32)

    scores(0, 0)

    def body(t, carry):
        scores(2 * t + 1, 1)
        softmax(0, None)
        pv(jnp.maximum(2 * t - 1, 0), 1)
        scores(2 * t + 2, 0)
        softmax(1, None)
        pv(2 * t, 0)
        return carry

    lax.fori_loop(0, i, body, 0)
    scores(2 * i + 1, 1)
    softmax(0, bias_ref[0])
    pv(jnp.maximum(2 * i - 1, 0), 1)
    softmax(1, bias_ref[1])
    pv(2 * i, 0)
    pv(2 * i + 1, 1)
    o_ref[...] = (acc_ref[...] * (1.0 / l_ref[...])).T.astype(BF16)


def _attn_bias(tq, tk):
    key_chunk = jnp.arange(tq)[:, None] // CHUNK
    query_chunk = jnp.arange(tq)[None, :] // CHUNK
    bias = jnp.where(key_chunk <= query_chunk, 0.0, -jnp.inf).astype(F32)
    return bias.reshape(tq // tk, tk, tq)


def _attn(qt, k, vt, *, batch, seq, tq, tk):
    nq = seq // tq
    bias = _attn_bias(tq, tk)
    return pl.pallas_call(
        functools.partial(_attn_kernel, tq=tq, tk=tk),
        grid=(batch, MLA_HEADS, nq),
        in_specs=[
            pl.BlockSpec((None, MLA_QK, tq), lambda b, h, i: (h, 0, b * nq + i)),
            pl.BlockSpec((None, seq, MLA_QK), lambda b, h, i: (h, b, 0)),
            pl.BlockSpec((None, V_HEAD, seq), lambda b, h, i: (h, 0, b)),
            _resident(bias.shape),
        ],
        out_specs=pl.BlockSpec((tq, V_HEAD), lambda b, h, i: (b * nq + i, h)),
        out_shape=jax.ShapeDtypeStruct((batch * seq, MLA_HEADS * V_HEAD), BF16),
        scratch_shapes=[pltpu.VMEM((1, tq), F32), pltpu.VMEM((1, tq), F32), pltpu.VMEM((V_HEAD, tq), F32),
                        pltpu.VMEM((2, tk, tq), F32), pltpu.VMEM((2, tk, tq), BF16),
                        pltpu.VMEM((2, 1, tq), F32)],
        compiler_params=_params(("parallel", "parallel", "arbitrary")),
        name="attn_prompt",
    )(qt, k, vt, bias)


def _attn_sample_kernel(q_ref, cn_ref, krn_ref, cc_ref, krc_ref, wukt_ref, wuv_ref, o_ref, ql_ref, qr_ref,
                        *, tq):
    for h in range(MLA_HEADS):
        qh = q_ref[h]
        ql_ref[h * tq:(h + 1) * tq, :] = jnp.dot(
            qh[:, :QK_NOPE], wukt_ref[h], preferred_element_type=F32).astype(BF16)
        qr_ref[h * tq:(h + 1) * tq, :] = qh[:, QK_NOPE:]
    ql = ql_ref[...]
    qr = qr_ref[...]
    cc = cc_ref[...].astype(BF16)
    krc = krc_ref[...].astype(BF16)
    cn = cn_ref[...].astype(BF16)
    krn = krn_ref[...].astype(BF16)
    s_c = (lax.dot_general(ql, cc, _NT, preferred_element_type=F32)
           + lax.dot_general(qr, krc, _NT, preferred_element_type=F32))
    s_n = (lax.dot_general(ql, cn, _NT, preferred_element_type=F32)
           + lax.dot_general(qr, krn, _NT, preferred_element_type=F32))
    m = jnp.maximum(jnp.max(s_c, axis=-1, keepdims=True), jnp.max(s_n, axis=-1, keepdims=True))
    p_c = jnp.exp2(s_c - m)
    p_n = jnp.exp2(s_n - m)
    l = jnp.sum(p_c, axis=-1, keepdims=True) + jnp.sum(p_n, axis=-1, keepdims=True)
    ol = (jnp.dot(p_c.astype(BF16), cc, preferred_element_type=F32)
          + jnp.dot(p_n.astype(BF16), cn, preferred_element_type=F32)) / l
    ol = ol.astype(BF16)
    for h in range(MLA_HEADS):
        o_ref[:, h * V_HEAD:(h + 1) * V_HEAD] = jnp.dot(
            ol[h * tq:(h + 1) * tq], wuv_ref[h], preferred_element_type=F32).astype(BF16)


def _attn_sample(q, c_new, kr_new, cache_c, cache_kr, layer, wukt, wuv, *, batch, tq):
    past = cache_c.shape[2]
    return pl.pallas_call(
        functools.partial(_attn_sample_kernel, tq=tq),
        grid=(batch,),
        in_specs=[
            pl.BlockSpec((MLA_HEADS, tq, MLA_QK), lambda b: (0, b, 0)),
            pl.BlockSpec((tq, KV_LORA), lambda b: (b, 0)),
            pl.BlockSpec((tq, QK_ROPE), lambda b: (b, 0)),
            pl.BlockSpec((None, None, past, KV_LORA), lambda b: (layer, b, 0, 0)),
            pl.BlockSpec((None, None, past, QK_ROPE), lambda b: (layer, b, 0, 0)),
            _resident(wukt.shape),
            _resident(wuv.shape),
        ],
        out_specs=pl.BlockSpec((tq, MLA_HEADS * V_HEAD), lambda b: (b, 0)),
        out_shape=jax.ShapeDtypeStruct((batch * tq, MLA_HEADS * V_HEAD), BF16),
        scratch_shapes=[pltpu.VMEM((MLA_HEADS * tq, KV_LORA), BF16), pltpu.VMEM((MLA_HEADS * tq, QK_ROPE), BF16)],
        compiler_params=_params(("parallel",)),
        name="attn_sample",
    )(q, c_new, kr_new, cache_c, cache_kr, wukt, wuv)


def _ret_kernel(*refs, seq, cs, hps, has_state):
    if has_state:
        q_ref, k_ref, v_ref, dm_ref, qd_ref, kd_ref, cd_ref, s0_ref, o_ref, s_out = refs
    else:
        q_ref, k_ref, v_ref, dm_ref, qd_ref, kd_ref, cd_ref, o_ref, s_out = refs
    for h in range(hps):
        dm = dm_ref[h]
        qd = qd_ref[h]
        kd = kd_ref[h]
        cd = cd_ref[h]
        kcol = slice(h * RET_DK, (h + 1) * RET_DK)
        vcol = slice(h * RET_DV, (h + 1) * RET_DV)
        state = s0_ref[h] if has_state else jnp.zeros((RET_DK, RET_DV), F32)
        for c in range(seq // cs):
            rows = slice(c * cs, (c + 1) * cs)
            qc = q_ref[rows, kcol]
            kc = k_ref[rows, kcol]
            vc = v_ref[rows, vcol]
            att = lax.dot_general(qc, kc, _NT, preferred_element_type=F32) * dm
            o = (jnp.dot(att.astype(BF16), vc, preferred_element_type=F32)
                 + jnp.dot(qc, state.astype(BF16), preferred_element_type=F32) * qd)
            kdec = (kc.astype(F32) * kd).astype(BF16)
            state = state * cd + lax.dot_general(kdec, vc, _TN, preferred_element_type=F32)
            mu = jnp.mean(o, axis=-1, keepdims=True)
            d = o - mu
            var = jnp.mean(d * d, axis=-1, keepdims=True)
            o_ref[rows, vcol] = (d * lax.rsqrt(var + GN_EPS)).astype(BF16)
        s_out[h] = state


def _ret_tables(cs):
    lg = jnp.log1p(-jnp.exp2(-5.0 - jnp.arange(RET_HEADS, dtype=F32)))
    j = jnp.arange(cs, dtype=F32)
    diff = j[:, None] - j[None, :]
    dmask = jnp.where(diff[None] >= 0, jnp.exp(jnp.maximum(diff, 0.0)[None] * lg[:, None, None]), 0.0)
    q_dec = jnp.exp((j + 1.0)[None] * lg[:, None])[..., None]
    k_dec = jnp.exp((cs - 1.0 - j)[None] * lg[:, None])[..., None]
    c_dec = jnp.exp(cs * lg)[:, None, None]
    return (dmask,
            jnp.broadcast_to(q_dec, (RET_HEADS, cs, RET_DV)),
            jnp.broadcast_to(k_dec, (RET_HEADS, cs, RET_DK)),
            jnp.broadcast_to(c_dec, (RET_HEADS, 1, RET_DV)))


def _retention(qk, v, tables, state, layer, *, batch, seq, cs, hps):
    dmask, q_dec, k_dec, c_dec = tables
    k_col = RET_HEADS // hps
    head = lambda b, h: (h, 0, 0)
    in_specs = [
        pl.BlockSpec((seq, hps * RET_DK), lambda b, h: (b, h)),
        pl.BlockSpec((seq, hps * RET_DK), lambda b, h: (b, k_col + h)),
        pl.BlockSpec((seq, hps * RET_DV), lambda b, h: (b, h)),
        pl.BlockSpec((hps, cs, cs), head),
        pl.BlockSpec((hps, cs, RET_DV), head),
        pl.BlockSpec((hps, cs, RET_DK), head),
        pl.BlockSpec((hps, 1, RET_DV), head),
    ]
    args = [qk, qk, v, dmask, q_dec, k_dec, c_dec]
    if state is not None:
        in_specs.append(pl.BlockSpec((None, None, hps, RET_DK, RET_DV), lambda b, h: (layer, b, h, 0, 0)))
        args.append(state)
    return pl.pallas_call(
        functools.partial(_ret_kernel, seq=seq, cs=cs, hps=hps, has_state=state is not None),
        grid=(batch, RET_HEADS // hps),
        in_specs=in_specs,
        out_specs=[
            pl.BlockSpec((seq, hps * RET_DV), lambda b, h: (b, h)),
            pl.BlockSpec((None, hps, RET_DK, RET_DV), lambda b, h: (b, h, 0, 0)),
        ],
        out_shape=[
            jax.ShapeDtypeStruct((batch * seq, RET_V_W), BF16),
            jax.ShapeDtypeStruct((batch, RET_HEADS, RET_DK, RET_DV), F32),
        ],
        compiler_params=_params(("parallel", "arbitrary")),
        name="retention",
    )(*args)


def _merge_kernel(a_ref, ro_ref, rg_ref, gm_ref, gr_ref, x_ref, wm_ref, wr_ref, wo_ref, o_ref):
    a = jnp.dot(a_ref[...], wm_ref[...], preferred_element_type=F32)
    r = jnp.dot(rg_ref[...] * ro_ref[...], wr_ref[...], preferred_element_type=F32)
    mix = (gm_ref[...].astype(F32) * a + gr_ref[...].astype(F32) * r).astype(BF16)
    o_ref[...] = x_ref[...] + jnp.dot(mix, wo_ref[...], preferred_element_type=F32)


def _merge(a, ro, rg, gates, x, wm, wr, wo, *, tm):
    m = x.shape[0]
    row = lambda i: (i, 0)
    return pl.pallas_call(
        _merge_kernel,
        grid=(m // tm,),
        in_specs=[
            pl.BlockSpec((tm, MLA_HEADS * V_HEAD), row),
            pl.BlockSpec((tm, RET_V_W), row),
            pl.BlockSpec((tm, RET_V_W), row),
            pl.BlockSpec((tm, D_MODEL), lambda i: (i, 0)),
            pl.BlockSpec((tm, D_MODEL), lambda i: (i, 1)),
            pl.BlockSpec((tm, D_MODEL), row),
            _resident(wm.shape),
            _resident(wr.shape),
            _resident(wo.shape),
        ],
        out_specs=pl.BlockSpec((tm, D_MODEL), row),
        out_shape=jax.ShapeDtypeStruct((m, D_MODEL), F32),
        compiler_params=_params(("parallel",)),
        name="merge",
    )(a, ro, rg, gates, gates, x, wm, wr, wo)


def _rope_tables(pos, d):
    inv = ROPE_THETA ** (-jnp.arange(0, d, 2, dtype=F32) / d)
    ang = pos.astype(F32)[:, None] * inv[None, :]
    cos, sin = jnp.cos(ang), jnp.sin(ang)
    return jnp.concatenate([cos, cos], axis=-1), jnp.concatenate([-sin, sin], axis=-1)


def _half_swap(w):
    d = w.shape[-1]
    return jnp.concatenate([w[..., d // 2:], w[..., :d // 2]], axis=-1)


def _layer_weights(l, w_in, w_uq, w_uk, w_uv, w_mla_out, w_ret_out, w_out):
    wi = w_in[l]
    w_kr = wi[:, Q_LORA + KV_LORA:MLA_IN_W]
    wa = jnp.concatenate([wi[:, :MLA_IN_W], _half_swap(w_kr)], axis=1).astype(BF16)
    wq = w_uq[l].reshape(Q_LORA, MLA_HEADS, MLA_QK)
    wq_r = wq[:, :, QK_NOPE:]
    wuq = jnp.concatenate([
        wq[:, :, :QK_NOPE].reshape(Q_LORA, -1),
        wq_r.reshape(Q_LORA, -1),
        _half_swap(wq_r).reshape(Q_LORA, -1)], axis=1).astype(BF16)
    wuk = w_uk[l].reshape(KV_LORA, -1).astype(BF16)
    wuv = w_uv[l].reshape(KV_LORA, -1).astype(BF16)
    return dict(
        wa=wa, wrest=wi[:, MLA_IN_W:].astype(BF16), wuq=wuq, wuqt=wuq.T, wuk=wuk, wuvt=wuv.T,
        wukt=jnp.transpose(w_uk[l], (1, 2, 0)).astype(BF16),
        wuv=jnp.transpose(w_uv[l], (1, 0, 2)).astype(BF16),
        w_mla_out=w_mla_out[l].astype(BF16), w_ret_out=w_ret_out[l].astype(BF16), w_out=w_out[l].astype(BF16),
    )


def _tables(pos, transposed):
    c64, s64 = _rope_tables(pos, QK_ROPE)
    c128, s128 = _rope_tables(pos, RET_DK)
    tab = dict(krt=jnp.concatenate([c64, s64], axis=1), c128=c128, s128=s128)
    if transposed:
        tab.update(cq=c64.T, sq=s64.T)
    else:
        tab.update(cq=jnp.concatenate([c64, c64], axis=1), sq=jnp.concatenate([s64, s64], axis=1))
    return tab


def _mixer(h, w, g_mix, g_q, g_kv, tab, *, tm_proj, tm_mm, transposed):
    wuq = w["wuqt"] if transposed else w["wuq"]
    outs = _mla_proj(h, g_mix, w["wa"], g_q, g_kv, wuq, w["wuk"], w["wuvt"], tab["krt"], tab["cq"], tab["sq"],
                     tm=tm_proj, transposed=transposed)
    u = outs[0]
    tn = 1024
    wr = w["wrest"]
    qk = _mm(u, wr, 0, 2 * RET_QK_W, "rope", (tab["c128"], tab["s128"]), tm=tm_mm, tn=tn)
    v = _mm(u, wr, 2 * RET_QK_W, RET_V_W, "none", tm=tm_mm, tn=tn)
    rg = _mm(u, wr, 2 * RET_QK_W + RET_V_W, RET_V_W, "silu", tm=tm_mm, tn=tn)
    gates = _mm(u, wr, 2 * RET_QK_W + 2 * RET_V_W, 2 * D_MODEL, "sigmoid", tm=tm_mm, tn=tn)
    return outs[1:], qk, v, rg, gates


def kernel(x_prompt, x_sample, cache_ckv, cache_krope, state_ret, ffn1_norm, ffn1_w13, ffn1_w2, mix_norm, w_in,
           q_norm, kv_norm, w_uq, w_uk, w_uv, w_mla_out, w_ret_out, w_out, ffn2_norm, ffn2_w13, ffn2_w2,
           final_norm):
    batch, seq, _ = x_prompt.shape
    dbatch, dseq, _ = x_sample.shape
    past = cache_ckv.shape[2]
    mp, ms = batch * seq, dbatch * dseq

    tab_p = _tables(jnp.arange(seq), True)
    tab_s = _tables(jnp.tile(past + jnp.arange(dseq), dbatch), False)
    ret_tab_p = _ret_tables(256)
    ret_tab_s = _ret_tables(dseq)

    hp = x_prompt.reshape(mp, D_MODEL)
    hs = x_sample.reshape(ms, D_MODEL)
    outs = {k: [] for k in ("ckv_p", "kr_p", "ret_p", "ckv_s", "kr_s", "ret_s")}
    row = lambda v: v.reshape(1, -1)

    gf = row(final_norm)
    for l in range(DEPTH):
        w = _layer_weights(l, w_in, w_uq, w_uk, w_uv, w_mla_out, w_ret_out, w_out)
        g1, gm, gq, gkv, g2 = (row(ffn1_norm[l]), row(mix_norm[l]), row(q_norm[l]), row(kv_norm[l]),
                               row(ffn2_norm[l]))
        g_last = gf if l == DEPTH - 1 else None

        hs, w1, w3, w2 = _ffn(hs, g1, ffn1_w13, ffn1_w13, ffn1_w2, tm=ms, tf=512, cast_layer=l)
        hp = _ffn(hp, g1, w1, w3, w2, tm=512, tf=512)

        (q, ckv, kr), qk, v, rg, gates = _mixer(hs, w, gm, gq, gkv, tab_s,
                                                tm_proj=ms, tm_mm=ms, transposed=False)
        a = _attn_sample(q, ckv, kr, cache_ckv, cache_krope, l, w["wukt"], w["wuv"], batch=dbatch, tq=dseq)
        ro, s_new = _retention(qk, v, ret_tab_s, state_ret, l, batch=dbatch, seq=dseq, cs=dseq, hps=RET_HEADS)
        hs = _merge(a, ro, rg, gates, hs, w["w_mla_out"], w["w_ret_out"], w["w_out"], tm=ms)
        outs["ckv_s"].append(ckv.reshape(dbatch, dseq, KV_LORA))
        outs["kr_s"].append(kr.reshape(dbatch, dseq, QK_ROPE))
        outs["ret_s"].append(s_new)

        (qt, ckv, kr, k, vt), qk, v, rg, gates = _mixer(hp, w, gm, gq, gkv, tab_p,
                                                        tm_proj=256, tm_mm=1024, transposed=True)
        a = _attn(qt, k, vt, batch=batch, seq=seq, tq=512, tk=256)
        ro, s_new = _retention(qk, v, ret_tab_p, None, l, batch=batch, seq=seq, cs=256, hps=1)
        hp = _merge(a, ro, rg, gates, hp, w["w_mla_out"], w["w_ret_out"], w["w_out"], tm=256)
        outs["ckv_p"].append(ckv.reshape(batch, seq, KV_LORA))
        outs["kr_p"].append(kr.reshape(batch, seq, QK_ROPE))
        outs["ret_p"].append(s_new)

        hs, w1, w3, w2 = _ffn(hs, g2, ffn2_w13, ffn2_w13, ffn2_w2, g_last, tm=ms, tf=512, cast_layer=l)
        hp = _ffn(hp, g2, w1, w3, w2, g_last, tm=512, tf=512)

    y_prompt = hp.reshape(batch, seq, D_MODEL)
    y_sample = hs.reshape(dbatch, dseq, D_MODEL)
    return (y_prompt, y_sample, jnp.stack(outs["ckv_p"]), jnp.stack(outs["kr_p"]), jnp.stack(outs["ret_p"]),
            jnp.stack(outs["ckv_s"]), jnp.stack(outs["kr_s"]), jnp.stack(outs["ret_s"]))
```

```python
import functools
import math

import jax
import jax.numpy as jnp
from jax import lax
from jax.experimental import pallas as pl
from jax.experimental.pallas import tpu as pltpu

D_MODEL = 2048
DEPTH = 2
CHUNK = 64
MLA_HEADS = 8
QK_NOPE = 128
QK_ROPE = 64
V_HEAD = 128
Q_LORA = 512
KV_LORA = 512
MLA_QK = QK_NOPE + QK_ROPE
MLA_SCALE = MLA_QK ** -0.5
RET_HEADS = 8
RET_DK = 128
RET_DV = 256
RET_QK_W = RET_HEADS * RET_DK
RET_V_W = RET_HEADS * RET_DV
RET_K_SCALE = RET_DK ** -0.5
D_FF = 5632
ROPE_THETA = 10000.0
NORM_EPS = 1e-6
GN_EPS = 1e-5

MLA_IN_W = Q_LORA + KV_LORA + QK_ROPE
NOPE_W = MLA_HEADS * QK_NOPE
ROPE_W = MLA_HEADS * QK_ROPE
Q_SCALE = MLA_SCALE * math.log2(math.e)

LANE = 128
VMEM_LIMIT = 56 * 1024 * 1024

BF16 = jnp.bfloat16
F32 = jnp.float32
_NT = (((1,), (1,)), ((), ()))
_TN = (((0,), (0,)), ((), ()))


def _params(sem):
    return pltpu.CompilerParams(dimension_semantics=sem, vmem_limit_bytes=VMEM_LIMIT)


def _rms(x, g):
    return x * lax.rsqrt(jnp.mean(x * x, axis=-1, keepdims=True) + NORM_EPS) * g


def _resident(shape):
    nd = len(shape)
    return pl.BlockSpec(shape, lambda *_: (0,) * nd, pipeline_mode=pl.Buffered(1))


def _ffn_kernel(*refs, cast, final):
    n_in = 6 if final else 5
    x_ref, g_ref, w1_ref, w3_ref, w2_ref = refs[:5]
    o_ref = refs[n_in]
    xn_ref = refs[-1]
    j = pl.program_id(1)

    @pl.when(j == 0)
    def _():
        x = x_ref[...]
        xn_ref[...] = _rms(x, g_ref[...]).astype(BF16)
        o_ref[...] = x

    w1, w3, w2 = w1_ref[...], w3_ref[...], w2_ref[...]
    if cast:
        w1o_ref, w3o_ref, w2o_ref = refs[n_in + 1:n_in + 4]
        w1, w3, w2 = w1.astype(BF16), w3.astype(BF16), (0.5 * w2).astype(BF16)
        w1o_ref[...] = w1
        w3o_ref[...] = w3
        w2o_ref[...] = w2
    xn = xn_ref[...]
    a = jnp.dot(xn, w1, preferred_element_type=F32)
    b = jnp.dot(xn, w3, preferred_element_type=F32)
    h = (a * jax.nn.sigmoid(a) * b).astype(BF16)
    o_ref[...] += jnp.dot(h, w2, preferred_element_type=F32)

    if final:
        @pl.when(j == pl.num_programs(1) - 1)
        def _():
            o_ref[...] = _rms(o_ref[...], refs[5][...])


def _ffn(x, g, w1, w3, w2, gf=None, *, tm, tf, cast_layer=None):
    m = x.shape[0]
    nf = D_FF // tf
    final = gf is not None
    cast = cast_layer is not None
    vec = pl.BlockSpec((1, D_MODEL), lambda i, j: (0, 0))
    w_up = lambda off: pl.BlockSpec((D_MODEL, tf), lambda i, j: (0, j + off))
    w_down = pl.BlockSpec((tf, D_MODEL), lambda i, j: (j, 0))
    if cast:
        w_up_in = lambda off: pl.BlockSpec((None, D_MODEL, tf), lambda i, j: (cast_layer, 0, j + off))
        w_in_specs = [w_up_in(0), w_up_in(nf), pl.BlockSpec((None, tf, D_MODEL), lambda i, j: (cast_layer, j, 0))]
    else:
        w_in_specs = [w_up(0), w_up(0), w_down]
    in_specs = [pl.BlockSpec((tm, D_MODEL), lambda i, j: (i, 0)), vec] + w_in_specs
    args = [x, g, w1, w3, w2]
    if final:
        in_specs.append(vec)
        args.append(gf)
    out_specs = [pl.BlockSpec((tm, D_MODEL), lambda i, j: (i, 0))]
    out_shape = [jax.ShapeDtypeStruct((m, D_MODEL), F32)]
    if cast:
        assert m == tm, "weights must stream through exactly once when their copies are emitted"
        out_specs += [w_up(0), w_up(0), w_down]
        out_shape += [jax.ShapeDtypeStruct((D_MODEL, D_FF), BF16), jax.ShapeDtypeStruct((D_MODEL, D_FF), BF16),
                      jax.ShapeDtypeStruct((D_FF, D_MODEL), BF16)]
    outs = pl.pallas_call(
        functools.partial(_ffn_kernel, cast=cast, final=final),
        grid=(m // tm, nf),
        in_specs=in_specs,
        out_specs=out_specs,
        out_shape=out_shape,
        scratch_shapes=[pltpu.VMEM((tm, D_MODEL), BF16)],
        compiler_params=_params(("parallel", "arbitrary")),
        name="ffn_cast" if cast else "ffn",
    )(*args)
    return outs if cast else outs[0]


def _mla_proj_kernel(x_ref, g_ref, wa_ref, qg_ref, kvg_ref, wuq_ref, wuk_ref, wuvt_ref, krt_ref, cq_ref, sq_ref,
                     u_out, q_out, ckv_out, kr_out, *kv_outs, transposed):
    u = _rms(x_ref[...], g_ref[...]).astype(BF16)
    u_out[...] = u
    t = jnp.dot(u, wa_ref[...], preferred_element_type=F32)
    qn = _rms(t[:, :Q_LORA], qg_ref[...])
    c = _rms(t[:, Q_LORA:Q_LORA + KV_LORA], kvg_ref[...])
    ckv_out[...] = c
    z = t[:, Q_LORA + KV_LORA:] * krt_ref[...]
    kr = (z + pltpu.roll(z, QK_ROPE, 1))[:, :QK_ROPE]
    kr_out[...] = kr
    cq = cq_ref[...]
    sq = sq_ref[...]
    if transposed:
        tq = jnp.dot(wuq_ref[...], qn.T.astype(BF16), preferred_element_type=F32)
        for h in range(MLA_HEADS):
            q_out[h, :QK_NOPE, :] = (tq[h * QK_NOPE:(h + 1) * QK_NOPE] * Q_SCALE).astype(BF16)
            lo = NOPE_W + h * QK_ROPE
            qr = tq[lo:lo + QK_ROPE] * cq + tq[lo + ROPE_W:lo + ROPE_W + QK_ROPE] * sq
            q_out[h, QK_NOPE:, :] = (qr * Q_SCALE).astype(BF16)
        k_out, vt_out = kv_outs
        kn = jnp.dot(c.astype(BF16), wuk_ref[...], preferred_element_type=F32)
        vt = jnp.dot(wuvt_ref[...], c.T.astype(BF16), preferred_element_type=F32)
        krb = kr.astype(BF16)
        for h in range(MLA_HEADS):
            k_out[h, :, :QK_NOPE] = kn[:, h * QK_NOPE:(h + 1) * QK_NOPE].astype(BF16)
            k_out[h, :, QK_NOPE:] = krb
            vt_out[h] = vt[h * V_HEAD:(h + 1) * V_HEAD].astype(BF16)
    else:
        tq = jnp.dot(qn.astype(BF16), wuq_ref[...], preferred_element_type=F32)
        for pair in range(MLA_HEADS // 2):
            lo = NOPE_W + pair * LANE
            qr = (tq[:, lo:lo + LANE] * cq + tq[:, lo + ROPE_W:lo + ROPE_W + LANE] * sq) * Q_SCALE
            for e in range(2):
                h = 2 * pair + e
                q_out[h, :, :QK_NOPE] = (tq[:, h * QK_NOPE:(h + 1) * QK_NOPE] * Q_SCALE).astype(BF16)
                q_out[h, :, QK_NOPE:] = qr[:, e * QK_ROPE:(e + 1) * QK_ROPE].astype(BF16)


def _mla_proj(x, g, wa, qg, kvg, wuq, wuk, wuvt, krt, cq, sq, *, tm, transposed):
    m = x.shape[0]
    row = lambda i: (i, 0)
    out_specs = [pl.BlockSpec((tm, D_MODEL), row)]
    out_shape = [jax.ShapeDtypeStruct((m, D_MODEL), BF16)]
    if transposed:
        nt = krt.shape[0] // tm
        q_tab = pl.BlockSpec((QK_ROPE, tm), lambda i: (0, i % nt))
        out_specs.append(pl.BlockSpec((MLA_HEADS, MLA_QK, tm), lambda i: (0, 0, i)))
        out_shape.append(jax.ShapeDtypeStruct((MLA_HEADS, MLA_QK, m), BF16))
    else:
        nt = krt.shape[0] // tm
        q_tab = pl.BlockSpec((tm, LANE), lambda i: (i % nt, 0))
        out_specs.append(pl.BlockSpec((MLA_HEADS, tm, MLA_QK), lambda i: (0, i, 0)))
        out_shape.append(jax.ShapeDtypeStruct((MLA_HEADS, m, MLA_QK), BF16))
    out_specs += [pl.BlockSpec((tm, KV_LORA), row), pl.BlockSpec((tm, QK_ROPE), row)]
    out_shape += [jax.ShapeDtypeStruct((m, KV_LORA), F32), jax.ShapeDtypeStruct((m, QK_ROPE), F32)]
    if transposed:
        out_specs += [pl.BlockSpec((MLA_HEADS, tm, MLA_QK), lambda i: (0, i, 0)),
                      pl.BlockSpec((MLA_HEADS, V_HEAD, tm), lambda i: (0, 0, i))]
        out_shape += [jax.ShapeDtypeStruct((MLA_HEADS, m, MLA_QK), BF16),
                      jax.ShapeDtypeStruct((MLA_HEADS, V_HEAD, m), BF16)]
    return pl.pallas_call(
        functools.partial(_mla_proj_kernel, transposed=transposed),
        grid=(m // tm,),
        in_specs=[
            pl.BlockSpec((tm, D_MODEL), row),
            _resident((1, D_MODEL)),
            _resident(wa.shape),
            _resident((1, Q_LORA)),
            _resident((1, KV_LORA)),
            _resident(wuq.shape),
            _resident(wuk.shape),
            _resident(wuvt.shape),
            pl.BlockSpec((tm, LANE), lambda i: (i % nt, 0)),
            q_tab,
            q_tab,
        ],
        out_specs=out_specs,
        out_shape=out_shape,
        compiler_params=_params(("parallel",)),
        name="mla_proj",
    )(x, g, wa, qg, kvg, wuq, wuk, wuvt, krt, cq, sq)


MM_SUB = 512


def _mm_kernel(*refs, epilogue, tn):
    if epilogue == "rope":
        u_ref, w_ref, c_ref, s_ref, o_ref = refs
    else:
        u_ref, w_ref, o_ref = refs
    u = u_ref[...]
    for n in range(tn // MM_SUB):
        cols = slice(n * MM_SUB, (n + 1) * MM_SUB)
        acc = jnp.dot(u, w_ref[:, cols], preferred_element_type=F32)
        if epilogue == "rope":
            is_k = pl.program_id(1) * tn + n * MM_SUB >= RET_QK_W
            scale = jnp.where(is_k, RET_K_SCALE, 1.0).astype(F32)
            c = c_ref[...] * scale
            s = s_ref[...] * scale
            for h in range(MM_SUB // RET_DK):
                xh = acc[:, h * RET_DK:(h + 1) * RET_DK]
                lo = n * MM_SUB + h * RET_DK
                o_ref[:, lo:lo + RET_DK] = (xh * c + pltpu.roll(xh, RET_DK // 2, 1) * s).astype(BF16)
        elif epilogue == "silu":
            o_ref[:, cols] = (acc * jax.nn.sigmoid(acc)).astype(BF16)
        elif epilogue == "sigmoid":
            o_ref[:, cols] = jax.nn.sigmoid(acc).astype(BF16)
        else:
            o_ref[:, cols] = acc.astype(BF16)


def _mm(u, w, col0, ncols, epilogue, tables=None, *, tm, tn):
    m = u.shape[0]
    c0 = col0 // tn
    in_specs = [
        pl.BlockSpec((tm, D_MODEL), lambda i, j: (i, 0)),
        pl.BlockSpec((D_MODEL, tn), lambda i, j: (0, c0 + j)),
    ]
    args = [u, w]
    if epilogue == "rope":
        nt = tables[0].shape[0] // tm
        in_specs += [pl.BlockSpec((tm, LANE), lambda i, j: (i % nt, 0))] * 2
        args += list(tables)
    return pl.pallas_call(
        functools.partial(_mm_kernel, epilogue=epilogue, tn=tn),
        grid=(m // tm, ncols // tn),
        in_specs=in_specs,
        out_specs=pl.BlockSpec((tm, tn), lambda i, j: (i, j)),
        out_shape=jax.ShapeDtypeStruct((m, ncols), BF16),
        compiler_params=_params(("parallel", "arbitrary")),
        name="mm_" + epilogue,
    )(*args)


def _attn_kernel(qt_ref, k_ref, vt_ref, bias_ref, o_ref, m_ref, l_ref, acc_ref, s_ref, p_ref, a_ref, *, tq, tk):
    i = pl.program_id(2)
    assert tq == 2 * tk
    qt = qt_ref[...]
    m_ref[...] = jnp.full_like(m_ref, -jnp.inf)
    l_ref[...] = jnp.zeros_like(l_ref)
    acc_ref[...] = jnp.zeros_like(acc_ref)
    p_ref[1] = jnp.zeros((tk, tq), BF16)
    a_ref[1] = jnp.ones((1, tq), F32)

    def scores(j, slot):
        start = pl.multiple_of(j * tk, tk)
        s_ref[slot] = jnp.dot(k_ref[pl.ds(start, tk), :], qt, preferred_element_type=F32)

    def softmax(slot, bias):
        s = s_ref[slot]
        if bias is not None:
            s = s + bias
        m = m_ref[...]
        m_new = jnp.maximum(m, jnp.max(s, axis=0, keepdims=True))
        alpha = jnp.exp2(m - m_new)
        p = jnp.exp2(s - m_new)
        l_ref[...] = alpha * l_ref[...] + jnp.sum(p, axis=0, keepdims=True)
        m_ref[...] = m_new
        p_ref[slot] = p.astype(BF16)
        a_ref[slot] = alpha

    def pv(j, slot):
        start = pl.multiple_of(j * tk, tk)
        acc_ref[...] = a_ref[slot] * acc_ref[...] + jnp.dot(
            vt_ref[:, pl.ds(start, tk)], p_ref[slot], preferred_element_type=F32)

    scores(0, 0)

    def body(t, carry):
        scores(2 * t + 1, 1)
        softmax(0, None)
        pv(jnp.maximum(2 * t - 1, 0), 1)
        scores(2 * t + 2, 0)
        softmax(1, None)
        pv(2 * t, 0)
        return carry

    lax.fori_loop(0, i, body, 0)
    scores(2 * i + 1, 1)
    softmax(0, bias_ref[0])
    pv(jnp.maximum(2 * i - 1, 0), 1)
    softmax(1, bias_ref[1])
    pv(2 * i, 0)
    pv(2 * i + 1, 1)
    o_ref[...] = (acc_ref[...] * (1.0 / l_ref[...])).T.astype(BF16)


def _attn_bias(tq, tk):
    key_chunk = jnp.arange(tq)[:, None] // CHUNK
    query_chunk = jnp.arange(tq)[None, :] // CHUNK
    bias = jnp.where(key_chunk <= query_chunk, 0.0, -jnp.inf).astype(F32)
    return bias.reshape(tq // tk, tk, tq)


def _attn(qt, k, vt, *, batch, seq, tq, tk):
    nq = seq // tq
    bias = _attn_bias(tq, tk)
    return pl.pallas_call(
        functools.partial(_attn_kernel, tq=tq, tk=tk),
        grid=(batch, MLA_HEADS, nq),
        in_specs=[
            pl.BlockSpec((None, MLA_QK, tq), lambda b, h, i: (h, 0, b * nq + i)),
            pl.BlockSpec((None, seq, MLA_QK), lambda b, h, i: (h, b, 0)),
            pl.BlockSpec((None, V_HEAD, seq), lambda b, h, i: (h, 0, b)),
            _resident(bias.shape),
        ],
        out_specs=pl.BlockSpec((tq, V_HEAD), lambda b, h, i: (b * nq + i, h)),
        out_shape=jax.ShapeDtypeStruct((batch * seq, MLA_HEADS * V_HEAD), BF16),
        scratch_shapes=[pltpu.VMEM((1, tq), F32), pltpu.VMEM((1, tq), F32), pltpu.VMEM((V_HEAD, tq), F32),
                        pltpu.VMEM((2, tk, tq), F32), pltpu.VMEM((2, tk, tq), BF16),
                        pltpu.VMEM((2, 1, tq), F32)],
        compiler_params=_params(("parallel", "parallel", "arbitrary")),
        name="attn_prompt",
    )(qt, k, vt, bias)


def _attn_sample_kernel(q_ref, cn_ref, krn_ref, cc_ref, krc_ref, wukt_ref, wuv_ref, o_ref, ql_ref, qr_ref,
                        *, tq):
    for h in range(MLA_HEADS):
        qh = q_ref[h]
        ql_ref[h * tq:(h + 1) * tq, :] = jnp.dot(
            qh[:, :QK_NOPE], wukt_ref[h], preferred_element_type=F32).astype(BF16)
        qr_ref[h * tq:(h + 1) * tq, :] = qh[:, QK_NOPE:]
    ql = ql_ref[...]
    qr = qr_ref[...]
    cc = cc_ref[...].astype(BF16)
    krc_t = krc_ref[...].astype(BF16)
    cn = cn_ref[...].astype(BF16)
    krn = krn_ref[...].astype(BF16)
    s_c = (lax.dot_general(ql, cc, _NT, preferred_element_type=F32)
           + jnp.dot(qr, krc_t, preferred_element_type=F32))
    s_n = (lax.dot_general(ql, cn, _NT, preferred_element_type=F32)
           + lax.dot_general(qr, krn, _NT, preferred_element_type=F32))
    m = jnp.maximum(jnp.max(s_c, axis=-1, keepdims=True), jnp.max(s_n, axis=-1, keepdims=True))
    p_c = jnp.exp2(s_c - m)
    p_n = jnp.exp2(s_n - m)
    l = jnp.sum(p_c, axis=-1, keepdims=True) + jnp.sum(p_n, axis=-1, keepdims=True)
    ol = (jnp.dot(p_c.astype(BF16), cc, preferred_element_type=F32)
          + jnp.dot(p_n.astype(BF16), cn, preferred_element_type=F32)) / l
    ol = ol.astype(BF16)
    for h in range(MLA_HEADS):
        o_ref[:, h * V_HEAD:(h + 1) * V_HEAD] = jnp.dot(
            ol[h * tq:(h + 1) * tq], wuv_ref[h], preferred_element_type=F32).astype(BF16)


def _attn_sample(q, c_new, kr_new, cache_c, cache_kr, layer, wukt, wuv, *, batch, tq):
    past = cache_c.shape[2]
    return pl.pallas_call(
        functools.partial(_attn_sample_kernel, tq=tq),
        grid=(batch,),
        in_specs=[
            pl.BlockSpec((MLA_HEADS, tq, MLA_QK), lambda b: (0, b, 0)),
            pl.BlockSpec((tq, KV_LORA), lambda b: (b, 0)),
            pl.BlockSpec((tq, QK_ROPE), lambda b: (b, 0)),
            pl.BlockSpec((None, None, past, KV_LORA), lambda b: (layer, b, 0, 0)),
            pl.BlockSpec((None, None, QK_ROPE, past), lambda b: (layer, b, 0, 0)),
            _resident(wukt.shape),
            _resident(wuv.shape),
        ],
        out_specs=pl.BlockSpec((tq, MLA_HEADS * V_HEAD), lambda b: (b, 0)),
        out_shape=jax.ShapeDtypeStruct((batch * tq, MLA_HEADS * V_HEAD), BF16),
        scratch_shapes=[pltpu.VMEM((MLA_HEADS * tq, KV_LORA), BF16), pltpu.VMEM((MLA_HEADS * tq, QK_ROPE), BF16)],
        compiler_params=_params(("parallel",)),
        name="attn_sample",
    )(q, c_new, kr_new, cache_c, cache_kr, wukt, wuv)


def _ret_kernel(*refs, seq, cs, hps, has_state):
    if has_state:
        q_ref, k_ref, v_ref, dm_ref, qd_ref, kd_ref, cd_ref, s0_ref, o_ref, s_out = refs
    else:
        q_ref, k_ref, v_ref, dm_ref, qd_ref, kd_ref, cd_ref, o_ref, s_out = refs
    for h in range(hps):
        dm = dm_ref[h]
        qd = qd_ref[h]
        kd = kd_ref[h]
        cd = cd_ref[h]
        kcol = slice(h * RET_DK, (h + 1) * RET_DK)
        vcol = slice(h * RET_DV, (h + 1) * RET_DV)
        state = s0_ref[h] if has_state else jnp.zeros((RET_DK, RET_DV), F32)
        for c in range(seq // cs):
            rows = slice(c * cs, (c + 1) * cs)
            qc = q_ref[rows, kcol]
            kc = k_ref[rows, kcol]
            vc = v_ref[rows, vcol]
            att = lax.dot_general(qc, kc, _NT, preferred_element_type=F32) * dm
            o = (jnp.dot(att.astype(BF16), vc, preferred_element_type=F32)
                 + jnp.dot(qc, state.astype(BF16), preferred_element_type=F32) * qd)
            kdec = (kc.astype(F32) * kd).astype(BF16)
            state = state * cd + lax.dot_general(kdec, vc, _TN, preferred_element_type=F32)
            mu = jnp.mean(o, axis=-1, keepdims=True)
            d = o - mu
            var = jnp.mean(d * d, axis=-1, keepdims=True)
            o_ref[rows, vcol] = (d * lax.rsqrt(var + GN_EPS)).astype(BF16)
        s_out[h] = state


def _ret_tables(cs):
    lg = jnp.log1p(-jnp.exp2(-5.0 - jnp.arange(RET_HEADS, dtype=F32)))
    j = jnp.arange(cs, dtype=F32)
    diff = j[:, None] - j[None, :]
    dmask = jnp.where(diff[None] >= 0, jnp.exp(jnp.maximum(diff, 0.0)[None] * lg[:, None, None]), 0.0)
    q_dec = jnp.exp((j + 1.0)[None] * lg[:, None])[..., None]
    k_dec = jnp.exp((cs - 1.0 - j)[None] * lg[:, None])[..., None]
    c_dec = jnp.exp(cs * lg)[:, None, None]
    return (dmask,
            jnp.broadcast_to(q_dec, (RET_HEADS, cs, RET_DV)),
            jnp.broadcast_to(k_dec, (RET_HEADS, cs, RET_DK)),
            jnp.broadcast_to(c_dec, (RET_HEADS, 1, RET_DV)))


def _retention(qk, v, tables, state, layer, *, batch, seq, cs, hps):
    dmask, q_dec, k_dec, c_dec = tables
    k_col = RET_HEADS // hps
    head = lambda b, h: (h, 0, 0)
    in_specs = [
        pl.BlockSpec((seq, hps * RET_DK), lambda b, h: (b, h)),
        pl.BlockSpec((seq, hps * RET_DK), lambda b, h: (b, k_col + h)),
        pl.BlockSpec((seq, hps * RET_DV), lambda b, h: (b, h)),
        pl.BlockSpec((hps, cs, cs), head),
        pl.BlockSpec((hps, cs, RET_DV), head),
        pl.BlockSpec((hps, cs, RET_DK), head),
        pl.BlockSpec((hps, 1, RET_DV), head),
    ]
    args = [qk, qk, v, dmask, q_dec, k_dec, c_dec]
    if state is not None:
        in_specs.append(pl.BlockSpec((None, None, hps, RET_DK, RET_DV), lambda b, h: (layer, b, h, 0, 0)))
        args.append(state)
    return pl.pallas_call(
        functools.partial(_ret_kernel, seq=seq, cs=cs, hps=hps, has_state=state is not None),
        grid=(batch, RET_HEADS // hps),
        in_specs=in_specs,
        out_specs=[
            pl.BlockSpec((seq, hps * RET_DV), lambda b, h: (b, h)),
            pl.BlockSpec((None, hps, RET_DK, RET_DV), lambda b, h: (b, h, 0, 0)),
        ],
        out_shape=[
            jax.ShapeDtypeStruct((batch * seq, RET_V_W), BF16),
            jax.ShapeDtypeStruct((batch, RET_HEADS, RET_DK, RET_DV), F32),
        ],
        compiler_params=_params(("parallel", "arbitrary")),
        name="retention",
    )(*args)


def _merge_kernel(a_ref, ro_ref, rg_ref, gm_ref, gr_ref, x_ref, wm_ref, wr_ref, wo_ref, o_ref):
    a = jnp.dot(a_ref[...], wm_ref[...], preferred_element_type=F32)
    r = jnp.dot(rg_ref[...] * ro_ref[...], wr_ref[...], preferred_element_type=F32)
    mix = (gm_ref[...].astype(F32) * a + gr_ref[...].astype(F32) * r).astype(BF16)
    o_ref[...] = x_ref[...] + jnp.dot(mix, wo_ref[...], preferred_element_type=F32)


def _merge(a, ro, rg, gates, x, wm, wr, wo, *, tm):
    m = x.shape[0]
    row = lambda i: (i, 0)
    return pl.pallas_call(
        _merge_kernel,
        grid=(m // tm,),
        in_specs=[
            pl.BlockSpec((tm, MLA_HEADS * V_HEAD), row),
            pl.BlockSpec((tm, RET_V_W), row),
            pl.BlockSpec((tm, RET_V_W), row),
            pl.BlockSpec((tm, D_MODEL), lambda i: (i, 0)),
            pl.BlockSpec((tm, D_MODEL), lambda i: (i, 1)),
            pl.BlockSpec((tm, D_MODEL), row),
            _resident(wm.shape),
            _resident(wr.shape),
            _resident(wo.shape),
        ],
        out_specs=pl.BlockSpec((tm, D_MODEL), row),
        out_shape=jax.ShapeDtypeStruct((m, D_MODEL), F32),
        compiler_params=_params(("parallel",)),
        name="merge",
    )(a, ro, rg, gates, gates, x, wm, wr, wo)


def _rope_tables(pos, d):
    inv = ROPE_THETA ** (-jnp.arange(0, d, 2, dtype=F32) / d)
    ang = pos.astype(F32)[:, None] * inv[None, :]
    cos, sin = jnp.cos(ang), jnp.sin(ang)
    return jnp.concatenate([cos, cos], axis=-1), jnp.concatenate([-sin, sin], axis=-1)


def _half_swap(w):
    d = w.shape[-1]
    return jnp.concatenate([w[..., d // 2:], w[..., :d // 2]], axis=-1)


def _layer_weights(l, w_in, w_uq, w_uk, w_uv, w_mla_out, w_ret_out, w_out):
    wi = w_in[l]
    w_kr = wi[:, Q_LORA + KV_LORA:MLA_IN_W]
    wa = jnp.concatenate([wi[:, :MLA_IN_W], _half_swap(w_kr)], axis=1).astype(BF16)
    wq = w_uq[l].reshape(Q_LORA, MLA_HEADS, MLA_QK)
    wq_r = wq[:, :, QK_NOPE:]
    wuq = jnp.concatenate([
        wq[:, :, :QK_NOPE].reshape(Q_LORA, -1),
        wq_r.reshape(Q_LORA, -1),
        _half_swap(wq_r).reshape(Q_LORA, -1)], axis=1).astype(BF16)
    wuk = w_uk[l].reshape(KV_LORA, -1).astype(BF16)
    wuv = w_uv[l].reshape(KV_LORA, -1).astype(BF16)
    return dict(
        wa=wa, wrest=wi[:, MLA_IN_W:].astype(BF16), wuq=wuq, wuqt=wuq.T, wuk=wuk, wuvt=wuv.T,
        wukt=jnp.transpose(w_uk[l], (1, 2, 0)).astype(BF16),
        wuv=jnp.transpose(w_uv[l], (1, 0, 2)).astype(BF16),
        w_mla_out=w_mla_out[l].astype(BF16), w_ret_out=w_ret_out[l].astype(BF16), w_out=w_out[l].astype(BF16),
    )


def _tables(pos, transposed):
    c64, s64 = _rope_tables(pos, QK_ROPE)
    c128, s128 = _rope_tables(pos, RET_DK)
    tab = dict(krt=jnp.concatenate([c64, s64], axis=1), c128=c128, s128=s128)
    if transposed:
        tab.update(cq=c64.T, sq=s64.T)
    else:
        tab.update(cq=jnp.concatenate([c64, c64], axis=1), sq=jnp.concatenate([s64, s64], axis=1))
    return tab


def _mixer(h, w, g_mix, g_q, g_kv, tab, *, tm_proj, tm_mm, transposed):
    wuq = w["wuqt"] if transposed else w["wuq"]
    outs = _mla_proj(h, g_mix, w["wa"], g_q, g_kv, wuq, w["wuk"], w["wuvt"], tab["krt"], tab["cq"], tab["sq"],
                     tm=tm_proj, transposed=transposed)
    u = outs[0]
    tn = 2048
    wr = w["wrest"]
    qk = _mm(u, wr, 0, 2 * RET_QK_W, "rope", (tab["c128"], tab["s128"]), tm=tm_mm, tn=tn)
    v = _mm(u, wr, 2 * RET_QK_W, RET_V_W, "none", tm=tm_mm, tn=tn)
    rg = _mm(u, wr, 2 * RET_QK_W + RET_V_W, RET_V_W, "silu", tm=tm_mm, tn=tn)
    gates = _mm(u, wr, 2 * RET_QK_W + 2 * RET_V_W, 2 * D_MODEL, "sigmoid", tm=tm_mm, tn=tn)
    return outs[1:], qk, v, rg, gates


def kernel(x_prompt, x_sample, cache_ckv, cache_krope, state_ret, ffn1_norm, ffn1_w13, ffn1_w2, mix_norm, w_in,
           q_norm, kv_norm, w_uq, w_uk, w_uv, w_mla_out, w_ret_out, w_out, ffn2_norm, ffn2_w13, ffn2_w2,
           final_norm):
    batch, seq, _ = x_prompt.shape
    dbatch, dseq, _ = x_sample.shape
    past = cache_ckv.shape[2]
    mp, ms = batch * seq, dbatch * dseq

    tab_p = _tables(jnp.arange(seq), True)
    tab_s = _tables(jnp.tile(past + jnp.arange(dseq), dbatch), False)
    ret_tab_p = _ret_tables(256)
    ret_tab_s = _ret_tables(dseq)
    cache_krope_t = jnp.swapaxes(cache_krope, 2, 3)

    hp = x_prompt.reshape(mp, D_MODEL)
    hs = x_sample.reshape(ms, D_MODEL)
    outs = {k: [] for k in ("ckv_p", "kr_p", "ret_p", "ckv_s", "kr_s", "ret_s")}
    row = lambda v: v.reshape(1, -1)

    gf = row(final_norm)
    for l in range(DEPTH):
        w = _layer_weights(l, w_in, w_uq, w_uk, w_uv, w_mla_out, w_ret_out, w_out)
        g1, gm, gq, gkv, g2 = (row(ffn1_norm[l]), row(mix_norm[l]), row(q_norm[l]), row(kv_norm[l]),
                               row(ffn2_norm[l]))
        g_last = gf if l == DEPTH - 1 else None

        hs, w1, w3, w2 = _ffn(hs, g1, ffn1_w13, ffn1_w13, ffn1_w2, tm=ms, tf=512, cast_layer=l)
        hp = _ffn(hp, g1, w1, w3, w2, tm=1024, tf=512)

        (q, ckv, kr), qk, v, rg, gates = _mixer(hs, w, gm, gq, gkv, tab_s,
                                                tm_proj=ms, tm_mm=ms, transposed=False)
        a = _attn_sample(q, ckv, kr, cache_ckv, cache_krope_t, l, w["wukt"], w["wuv"], batch=dbatch, tq=dseq)
        ro, s_new = _retention(qk, v, ret_tab_s, state_ret, l, batch=dbatch, seq=dseq, cs=dseq, hps=RET_HEADS)
        hs = _merge(a, ro, rg, gates, hs, w["w_mla_out"], w["w_ret_out"], w["w_out"], tm=ms)
        outs["ckv_s"].append(ckv.reshape(dbatch, dseq, KV_LORA))
        outs["kr_s"].append(kr.reshape(dbatch, dseq, QK_ROPE))
        outs["ret_s"].append(s_new)

        (qt, ckv, kr, k, vt), qk, v, rg, gates = _mixer(hp, w, gm, gq, gkv, tab_p,
                                                        tm_proj=512, tm_mm=1024, transposed=True)
        a = _attn(qt, k, vt, batch=batch, seq=seq, tq=512, tk=256)
        ro, s_new = _retention(qk, v, ret_tab_p, None, l, batch=batch, seq=seq, cs=256, hps=1)
        hp = _merge(a, ro, rg, gates, hp, w["w_mla_out"], w["w_ret_out"], w["w_out"], tm=256)
        outs["ckv_p"].append(ckv.reshape(batch, seq, KV_LORA))
        outs["kr_p"].append(kr.reshape(batch, seq, QK_ROPE))
        outs["ret_p"].append(s_new)

        hs, w1, w3, w2 = _ffn(hs, g2, ffn2_w13, ffn2_w13, ffn2_w2, g_last, tm=ms, tf=512, cast_layer=l)
        hp = _ffn(hp, g2, w1, w3, w2, g_last, tm=1024, tf=512)

    y_prompt = hp.reshape(batch, seq, D_MODEL)
    y_sample = hs.reshape(dbatch, dseq, D_MODEL)
    return (y_prompt, y_sample, jnp.stack(outs["ckv_p"]), jnp.stack(outs["kr_p"]), jnp.stack(outs["ret_p"]),
            jnp.stack(outs["ckv_s"]), jnp.stack(outs["kr_s"]), jnp.stack(outs["ret_s"]))
```

```python
import functools
import math

import jax
import jax.numpy as jnp
from jax import lax
from jax.experimental import pallas as pl
from jax.experimental.pallas import tpu as pltpu

D_MODEL = 2048
DEPTH = 2
CHUNK = 64
MLA_HEADS = 8
QK_NOPE = 128
QK_ROPE = 64
V_HEAD = 128
Q_LORA = 512
KV_LORA = 512
MLA_QK = QK_NOPE + QK_ROPE
MLA_SCALE = MLA_QK ** -0.5
RET_HEADS = 8
RET_DK = 128
RET_DV = 256
RET_QK_W = RET_HEADS * RET_DK
RET_V_W = RET_HEADS * RET_DV
RET_K_SCALE = RET_DK ** -0.5
D_FF = 5632
ROPE_THETA = 10000.0
NORM_EPS = 1e-6
GN_EPS = 1e-5

MLA_IN_W = Q_LORA + KV_LORA + QK_ROPE
NOPE_W = MLA_HEADS * QK_NOPE
ROPE_W = MLA_HEADS * QK_ROPE
Q_SCALE = MLA_SCALE * math.log2(math.e)

LANE = 128
VMEM_LIMIT = 56 * 1024 * 1024

BF16 = jnp.bfloat16
F32 = jnp.float32
_NT = (((1,), (1,)), ((), ()))
_TN = (((0,), (0,)), ((), ()))


def _params(sem):
    return pltpu.CompilerParams(dimension_semantics=sem, vmem_limit_bytes=VMEM_LIMIT)


def _rms(x, g):
    return x * lax.rsqrt(jnp.mean(x * x, axis=-1, keepdims=True) + NORM_EPS) * g


def _resident(shape):
    nd = len(shape)
    return pl.BlockSpec(shape, lambda *_: (0,) * nd, pipeline_mode=pl.Buffered(1))


def _ffn_kernel(*refs, cast, final):
    n_in = 6 if final else 5
    x_ref, g_ref, w1_ref, w3_ref, w2_ref = refs[:5]
    o_ref = refs[n_in]
    xn_ref = refs[-1]
    j = pl.program_id(1)

    def swiglu(xn):
        w1, w3, w2 = w1_ref[...], w3_ref[...], w2_ref[...]
        if cast:
            w1o_ref, w3o_ref, w2o_ref = refs[n_in + 1:n_in + 4]
            w1, w3, w2 = w1.astype(BF16), w3.astype(BF16), (0.5 * w2).astype(BF16)
            w1o_ref[...] = w1
            w3o_ref[...] = w3
            w2o_ref[...] = w2
        a = jnp.dot(xn, w1, preferred_element_type=F32)
        b = jnp.dot(xn, w3, preferred_element_type=F32)
        h = (a * jax.nn.sigmoid(a) * b).astype(BF16)
        return jnp.dot(h, w2, preferred_element_type=F32)

    @pl.when(j == 0)
    def _():
        x = x_ref[...]
        xn = _rms(x, g_ref[...]).astype(BF16)
        xn_ref[...] = xn
        o_ref[...] = x + swiglu(xn)

    @pl.when(j > 0)
    def _():
        o_ref[...] += swiglu(xn_ref[...])

    if final:
        @pl.when(j == pl.num_programs(1) - 1)
        def _():
            o_ref[...] = _rms(o_ref[...], refs[5][...])


def _ffn(x, g, w1, w3, w2, gf=None, *, tm, tf, cast_layer=None):
    m = x.shape[0]
    nf = D_FF // tf
    final = gf is not None
    cast = cast_layer is not None
    vec = pl.BlockSpec((1, D_MODEL), lambda i, j: (0, 0))
    w_up = lambda off: pl.BlockSpec((D_MODEL, tf), lambda i, j: (0, j + off))
    w_down = pl.BlockSpec((tf, D_MODEL), lambda i, j: (j, 0))
    if cast:
        w_up_in = lambda off: pl.BlockSpec((None, D_MODEL, tf), lambda i, j: (cast_layer, 0, j + off))
        w_in_specs = [w_up_in(0), w_up_in(nf), pl.BlockSpec((None, tf, D_MODEL), lambda i, j: (cast_layer, j, 0))]
    else:
        w_in_specs = [w_up(0), w_up(0), w_down]
    in_specs = [pl.BlockSpec((tm, D_MODEL), lambda i, j: (i, 0)), vec] + w_in_specs
    args = [x, g, w1, w3, w2]
    if final:
        in_specs.append(vec)
        args.append(gf)
    out_specs = [pl.BlockSpec((tm, D_MODEL), lambda i, j: (i, 0))]
    out_shape = [jax.ShapeDtypeStruct((m, D_MODEL), F32)]
    if cast:
        assert m == tm, "weights must stream through exactly once when their copies are emitted"
        out_specs += [w_up(0), w_up(0), w_down]
        out_shape += [jax.ShapeDtypeStruct((D_MODEL, D_FF), BF16), jax.ShapeDtypeStruct((D_MODEL, D_FF), BF16),
                      jax.ShapeDtypeStruct((D_FF, D_MODEL), BF16)]
    outs = pl.pallas_call(
        functools.partial(_ffn_kernel, cast=cast, final=final),
        grid=(m // tm, nf),
        in_specs=in_specs,
        out_specs=out_specs,
        out_shape=out_shape,
        scratch_shapes=[pltpu.VMEM((tm, D_MODEL), BF16)],
        compiler_params=_params(("parallel", "arbitrary")),
        name="ffn_cast" if cast else "ffn",
    )(*args)
    return outs if cast else outs[0]


def _mla_proj_kernel(x_ref, g_ref, wa_ref, qg_ref, kvg_ref, wuq_ref, wuk_ref, wuvt_ref, krt_ref, cq_ref, sq_ref,
                     u_out, q_out, ckv_out, kr_out, *kv_outs, transposed):
    u = _rms(x_ref[...], g_ref[...]).astype(BF16)
    u_out[...] = u
    t = jnp.dot(u, wa_ref[...], preferred_element_type=F32)
    qn = _rms(t[:, :Q_LORA], qg_ref[...])
    c = _rms(t[:, Q_LORA:Q_LORA + KV_LORA], kvg_ref[...])
    ckv_out[...] = c
    z = t[:, Q_LORA + KV_LORA:] * krt_ref[...]
    kr = (z + pltpu.roll(z, QK_ROPE, 1))[:, :QK_ROPE]
    kr_out[...] = kr
    cq = cq_ref[...]
    sq = sq_ref[...]
    if transposed:
        tq = jnp.dot(wuq_ref[...], qn.T.astype(BF16), preferred_element_type=F32)
        for h in range(MLA_HEADS):
            q_out[h, :QK_NOPE, :] = (tq[h * QK_NOPE:(h + 1) * QK_NOPE] * Q_SCALE).astype(BF16)
            lo = NOPE_W + h * QK_ROPE
            qr = tq[lo:lo + QK_ROPE] * cq + tq[lo + ROPE_W:lo + ROPE_W + QK_ROPE] * sq
            q_out[h, QK_NOPE:, :] = (qr * Q_SCALE).astype(BF16)
        k_out, vt_out = kv_outs
        kn = jnp.dot(c.astype(BF16), wuk_ref[...], preferred_element_type=F32)
        vt = jnp.dot(wuvt_ref[...], c.T.astype(BF16), preferred_element_type=F32)
        krb = kr.astype(BF16)
        for h in range(MLA_HEADS):
            k_out[h, :, :QK_NOPE] = kn[:, h * QK_NOPE:(h + 1) * QK_NOPE].astype(BF16)
            k_out[h, :, QK_NOPE:] = krb
            vt_out[h] = vt[h * V_HEAD:(h + 1) * V_HEAD].astype(BF16)
    else:
        tq = jnp.dot(qn.astype(BF16), wuq_ref[...], preferred_element_type=F32)
        for pair in range(MLA_HEADS // 2):
            lo = NOPE_W + pair * LANE
            qr = (tq[:, lo:lo + LANE] * cq + tq[:, lo + ROPE_W:lo + ROPE_W + LANE] * sq) * Q_SCALE
            for e in range(2):
                h = 2 * pair + e
                q_out[h, :, :QK_NOPE] = (tq[:, h * QK_NOPE:(h + 1) * QK_NOPE] * Q_SCALE).astype(BF16)
                q_out[h, :, QK_NOPE:] = qr[:, e * QK_ROPE:(e + 1) * QK_ROPE].astype(BF16)


def _mla_proj(x, g, wa, qg, kvg, wuq, wuk, wuvt, krt, cq, sq, *, tm, transposed):
    m = x.shape[0]
    row = lambda i: (i, 0)
    out_specs = [pl.BlockSpec((tm, D_MODEL), row)]
    out_shape = [jax.ShapeDtypeStruct((m, D_MODEL), BF16)]
    if transposed:
        nt = krt.shape[0] // tm
        q_tab = pl.BlockSpec((QK_ROPE, tm), lambda i: (0, i % nt))
        out_specs.append(pl.BlockSpec((MLA_HEADS, MLA_QK, tm), lambda i: (0, 0, i)))
        out_shape.append(jax.ShapeDtypeStruct((MLA_HEADS, MLA_QK, m), BF16))
    else:
        nt = krt.shape[0] // tm
        q_tab = pl.BlockSpec((tm, LANE), lambda i: (i % nt, 0))
        out_specs.append(pl.BlockSpec((MLA_HEADS, tm, MLA_QK), lambda i: (0, i, 0)))
        out_shape.append(jax.ShapeDtypeStruct((MLA_HEADS, m, MLA_QK), BF16))
    out_specs += [pl.BlockSpec((tm, KV_LORA), row), pl.BlockSpec((tm, QK_ROPE), row)]
    out_shape += [jax.ShapeDtypeStruct((m, KV_LORA), F32), jax.ShapeDtypeStruct((m, QK_ROPE), F32)]
    if transposed:
        out_specs += [pl.BlockSpec((MLA_HEADS, tm, MLA_QK), lambda i: (0, i, 0)),
                      pl.BlockSpec((MLA_HEADS, V_HEAD, tm), lambda i: (0, 0, i))]
        out_shape += [jax.ShapeDtypeStruct((MLA_HEADS, m, MLA_QK), BF16),
                      jax.ShapeDtypeStruct((MLA_HEADS, V_HEAD, m), BF16)]
    return pl.pallas_call(
        functools.partial(_mla_proj_kernel, transposed=transposed),
        grid=(m // tm,),
        in_specs=[
            pl.BlockSpec((tm, D_MODEL), row),
            _resident((1, D_MODEL)),
            _resident(wa.shape),
            _resident((1, Q_LORA)),
            _resident((1, KV_LORA)),
            _resident(wuq.shape),
            _resident(wuk.shape),
            _resident(wuvt.shape),
            pl.BlockSpec((tm, LANE), lambda i: (i % nt, 0)),
            q_tab,
            q_tab,
        ],
        out_specs=out_specs,
        out_shape=out_shape,
        compiler_params=_params(("parallel",)),
        name="mla_proj",
    )(x, g, wa, qg, kvg, wuq, wuk, wuvt, krt, cq, sq)


MM_SUB = 512


def _mm_kernel(*refs, epilogue, tn):
    if epilogue == "rope":
        u_ref, w_ref, c_ref, s_ref, o_ref = refs
    else:
        u_ref, w_ref, o_ref = refs
    u = u_ref[...]
    for n in range(tn // MM_SUB):
        cols = slice(n * MM_SUB, (n + 1) * MM_SUB)
        acc = jnp.dot(u, w_ref[:, cols], preferred_element_type=F32)
        if epilogue == "rope":
            is_k = pl.program_id(1) * tn + n * MM_SUB >= RET_QK_W
            scale = jnp.where(is_k, RET_K_SCALE, 1.0).astype(F32)
            c = c_ref[...] * scale
            s = s_ref[...] * scale
            for h in range(MM_SUB // RET_DK):
                xh = acc[:, h * RET_DK:(h + 1) * RET_DK]
                lo = n * MM_SUB + h * RET_DK
                o_ref[:, lo:lo + RET_DK] = (xh * c + pltpu.roll(xh, RET_DK // 2, 1) * s).astype(BF16)
        elif epilogue == "silu":
            o_ref[:, cols] = (acc * jax.nn.sigmoid(acc)).astype(BF16)
        elif epilogue == "sigmoid":
            o_ref[:, cols] = jax.nn.sigmoid(acc).astype(BF16)
        else:
            o_ref[:, cols] = acc.astype(BF16)


def _mm(u, w, col0, ncols, epilogue, tables=None, *, tm, tn):
    m = u.shape[0]
    c0 = col0 // tn
    in_specs = [
        pl.BlockSpec((tm, D_MODEL), lambda i, j: (i, 0)),
        pl.BlockSpec((D_MODEL, tn), lambda i, j: (0, c0 + j)),
    ]
    args = [u, w]
    if epilogue == "rope":
        nt = tables[0].shape[0] // tm
        in_specs += [pl.BlockSpec((tm, LANE), lambda i, j: (i % nt, 0))] * 2
        args += list(tables)
    return pl.pallas_call(
        functools.partial(_mm_kernel, epilogue=epilogue, tn=tn),
        grid=(m // tm, ncols // tn),
        in_specs=in_specs,
        out_specs=pl.BlockSpec((tm, tn), lambda i, j: (i, j)),
        out_shape=jax.ShapeDtypeStruct((m, ncols), BF16),
        compiler_params=_params(("parallel", "arbitrary")),
        name="mm_" + epilogue,
    )(*args)


def _attn_kernel(qt_ref, k_ref, vt_ref, bias_ref, o_ref, m_ref, l_ref, acc_ref, s_ref, p_ref, a_ref, *, tq, tk, nq):
    assert tq == 2 * tk
    for i in range(nq):
        st = i % 2
        qt = qt_ref[:, i * tq:(i + 1) * tq]
        m_ref[st] = jnp.full((1, tq), -jnp.inf, F32)
        l_ref[st] = jnp.zeros((1, tq), F32)
        acc_ref[st] = jnp.zeros((V_HEAD, tq), F32)
        p_ref[st, 1] = jnp.zeros((tk, tq), BF16)
        a_ref[st, 1] = jnp.ones((1, tq), F32)

        def scores(j, slot):
            s_ref[st, slot] = jnp.dot(k_ref[j * tk:(j + 1) * tk, :], qt, preferred_element_type=F32)

        def softmax(slot, bias):
            s = s_ref[st, slot]
            if bias is not None:
                s = s + bias
            m = m_ref[st]
            m_new = jnp.maximum(m, jnp.max(s, axis=0, keepdims=True))
            alpha = jnp.exp2(m - m_new)
            p = jnp.exp2(s - m_new)
            l_ref[st] = alpha * l_ref[st] + jnp.sum(p, axis=0, keepdims=True)
            m_ref[st] = m_new
            p_ref[st, slot] = p.astype(BF16)
            a_ref[st, slot] = alpha

        def pv(j, slot):
            acc_ref[st] = a_ref[st, slot] * acc_ref[st] + jnp.dot(
                vt_ref[:, j * tk:(j + 1) * tk], p_ref[st, slot], preferred_element_type=F32)

        scores(0, 0)
        for t in range(i):
            scores(2 * t + 1, 1)
            softmax(0, None)
            pv(max(2 * t - 1, 0), 1)
            scores(2 * t + 2, 0)
            softmax(1, None)
            pv(2 * t, 0)
        scores(2 * i + 1, 1)
        softmax(0, bias_ref[0])
        pv(max(2 * i - 1, 0), 1)
        softmax(1, bias_ref[1])
        pv(2 * i, 0)
        pv(2 * i + 1, 1)
        o_ref[i * tq:(i + 1) * tq, :] = (acc_ref[st] * (1.0 / l_ref[st])).T.astype(BF16)


def _attn_bias(tq, tk):
    key_chunk = jnp.arange(tq)[:, None] // CHUNK
    query_chunk = jnp.arange(tq)[None, :] // CHUNK
    bias = jnp.where(key_chunk <= query_chunk, 0.0, -jnp.inf).astype(F32)
    return bias.reshape(tq // tk, tk, tq)


def _attn(qt, k, vt, *, batch, seq, tq, tk):
    nq = seq // tq
    bias = _attn_bias(tq, tk)
    return pl.pallas_call(
        functools.partial(_attn_kernel, tq=tq, tk=tk, nq=nq),
        grid=(batch, MLA_HEADS),
        in_specs=[
            pl.BlockSpec((None, MLA_QK, seq), lambda b, h: (h, 0, b)),
            pl.BlockSpec((None, seq, MLA_QK), lambda b, h: (h, b, 0)),
            pl.BlockSpec((None, V_HEAD, seq), lambda b, h: (h, 0, b)),
            _resident(bias.shape),
        ],
        out_specs=pl.BlockSpec((seq, V_HEAD), lambda b, h: (b, h)),
        out_shape=jax.ShapeDtypeStruct((batch * seq, MLA_HEADS * V_HEAD), BF16),
        scratch_shapes=[pltpu.VMEM((2, 1, tq), F32), pltpu.VMEM((2, 1, tq), F32), pltpu.VMEM((2, V_HEAD, tq), F32),
                        pltpu.VMEM((2, 2, tk, tq), F32), pltpu.VMEM((2, 2, tk, tq), BF16),
                        pltpu.VMEM((2, 2, 1, tq), F32)],
        compiler_params=_params(("parallel", "arbitrary")),
        name="attn_prompt",
    )(qt, k, vt, bias)


def _attn_sample_kernel(q_ref, cn_ref, krn_ref, cc_ref, krc_ref, wukt_ref, wuv_ref, o_ref, ql_ref, qr_ref,
                        *, tq):
    for h in range(MLA_HEADS):
        qh = q_ref[h]
        ql_ref[h * tq:(h + 1) * tq, :] = jnp.dot(
            qh[:, :QK_NOPE], wukt_ref[h], preferred_element_type=F32).astype(BF16)
        qr_ref[h * tq:(h + 1) * tq, :] = qh[:, QK_NOPE:]
    ql = ql_ref[...]
    qr = qr_ref[...]
    cc = cc_ref[...].astype(BF16)
    krc_t = krc_ref[...].astype(BF16)
    cn = cn_ref[...].astype(BF16)
    krn = krn_ref[...].astype(BF16)
    s_c = (lax.dot_general(ql, cc, _NT, preferred_element_type=F32)
           + jnp.dot(qr, krc_t, preferred_element_type=F32))
    s_n = (lax.dot_general(ql, cn, _NT, preferred_element_type=F32)
           + lax.dot_general(qr, krn, _NT, preferred_element_type=F32))
    m = jnp.maximum(jnp.max(s_c, axis=-1, keepdims=True), jnp.max(s_n, axis=-1, keepdims=True))
    p_c = jnp.exp2(s_c - m)
    p_n = jnp.exp2(s_n - m)
    l = jnp.sum(p_c, axis=-1, keepdims=True) + jnp.sum(p_n, axis=-1, keepdims=True)
    ol = (jnp.dot(p_c.astype(BF16), cc, preferred_element_type=F32)
          + jnp.dot(p_n.astype(BF16), cn, preferred_element_type=F32)) / l
    ol = ol.astype(BF16)
    for h in range(MLA_HEADS):
        o_ref[:, h * V_HEAD:(h + 1) * V_HEAD] = jnp.dot(
            ol[h * tq:(h + 1) * tq], wuv_ref[h], preferred_element_type=F32).astype(BF16)


def _attn_sample(q, c_new, kr_new, cache_c, cache_kr, layer, wukt, wuv, *, batch, tq):
    past = cache_c.shape[2]
    return pl.pallas_call(
        functools.partial(_attn_sample_kernel, tq=tq),
        grid=(batch,),
        in_specs=[
            pl.BlockSpec((MLA_HEADS, tq, MLA_QK), lambda b: (0, b, 0)),
            pl.BlockSpec((tq, KV_LORA), lambda b: (b, 0)),
            pl.BlockSpec((tq, QK_ROPE), lambda b: (b, 0)),
            pl.BlockSpec((None, None, past, KV_LORA), lambda b: (layer, b, 0, 0)),
            pl.BlockSpec((None, None, QK_ROPE, past), lambda b: (layer, b, 0, 0)),
            _resident(wukt.shape),
            _resident(wuv.shape),
        ],
        out_specs=pl.BlockSpec((tq, MLA_HEADS * V_HEAD), lambda b: (b, 0)),
        out_shape=jax.ShapeDtypeStruct((batch * tq, MLA_HEADS * V_HEAD), BF16),
        scratch_shapes=[pltpu.VMEM((MLA_HEADS * tq, KV_LORA), BF16), pltpu.VMEM((MLA_HEADS * tq, QK_ROPE), BF16)],
        compiler_params=_params(("parallel",)),
        name="attn_sample",
    )(q, c_new, kr_new, cache_c, cache_kr, wukt, wuv)


def _ret_kernel(*refs, seq, cs, hps, has_state):
    if has_state:
        q_ref, k_ref, v_ref, dm_ref, qd_ref, kd_ref, cd_ref, s0_ref, o_ref, s_out = refs
    else:
        q_ref, k_ref, v_ref, dm_ref, qd_ref, kd_ref, cd_ref, o_ref, s_out = refs
    for h in range(hps):
        dm = dm_ref[h]
        qd = qd_ref[h]
        kd = kd_ref[h]
        cd = cd_ref[h]
        kcol = slice(h * RET_DK, (h + 1) * RET_DK)
        vcol = slice(h * RET_DV, (h + 1) * RET_DV)
        state = s0_ref[h] if has_state else jnp.zeros((RET_DK, RET_DV), F32)
        for c in range(seq // cs):
            rows = slice(c * cs, (c + 1) * cs)
            qc = q_ref[rows, kcol]
            kc = k_ref[rows, kcol]
            vc = v_ref[rows, vcol]
            att = lax.dot_general(qc, kc, _NT, preferred_element_type=F32) * dm
            o = (jnp.dot(att.astype(BF16), vc, preferred_element_type=F32)
                 + jnp.dot(qc, state.astype(BF16), preferred_element_type=F32) * qd)
            kdec = (kc.astype(F32) * kd).astype(BF16)
            state = state * cd + lax.dot_general(kdec, vc, _TN, preferred_element_type=F32)
            mu = jnp.mean(o, axis=-1, keepdims=True)
            d = o - mu
            var = jnp.mean(d * d, axis=-1, keepdims=True)
            o_ref[rows, vcol] = (d * lax.rsqrt(var + GN_EPS)).astype(BF16)
        s_out[h] = state


def _ret_tables(cs):
    lg = jnp.log1p(-jnp.exp2(-5.0 - jnp.arange(RET_HEADS, dtype=F32)))
    j = jnp.arange(cs, dtype=F32)
    diff = j[:, None] - j[None, :]
    dmask = jnp.where(diff[None] >= 0, jnp.exp(jnp.maximum(diff, 0.0)[None] * lg[:, None, None]), 0.0)
    q_dec = jnp.exp((j + 1.0)[None] * lg[:, None])[..., None]
    k_dec = jnp.exp((cs - 1.0 - j)[None] * lg[:, None])[..., None]
    c_dec = jnp.exp(cs * lg)[:, None, None]
    return (dmask,
            jnp.broadcast_to(q_dec, (RET_HEADS, cs, RET_DV)),
            jnp.broadcast_to(k_dec, (RET_HEADS, cs, RET_DK)),
            jnp.broadcast_to(c_dec, (RET_HEADS, 1, RET_DV)))


def _retention(qk, v, tables, state, layer, *, batch, seq, cs, hps):
    dmask, q_dec, k_dec, c_dec = tables
    k_col = RET_HEADS // hps
    head = lambda b, h: (h, 0, 0)
    in_specs = [
        pl.BlockSpec((seq, hps * RET_DK), lambda b, h: (b, h)),
        pl.BlockSpec((seq, hps * RET_DK), lambda b, h: (b, k_col + h)),
        pl.BlockSpec((seq, hps * RET_DV), lambda b, h: (b, h)),
        pl.BlockSpec((hps, cs, cs), head),
        pl.BlockSpec((hps, cs, RET_DV), head),
        pl.BlockSpec((hps, cs, RET_DK), head),
        pl.BlockSpec((hps, 1, RET_DV), head),
    ]
    args = [qk, qk, v, dmask, q_dec, k_dec, c_dec]
    if state is not None:
        in_specs.append(pl.BlockSpec((None, None, hps, RET_DK, RET_DV), lambda b, h: (layer, b, h, 0, 0)))
        args.append(state)
    return pl.pallas_call(
        functools.partial(_ret_kernel, seq=seq, cs=cs, hps=hps, has_state=state is not None),
        grid=(batch, RET_HEADS // hps),
        in_specs=in_specs,
        out_specs=[
            pl.BlockSpec((seq, hps * RET_DV), lambda b, h: (b, h)),
            pl.BlockSpec((None, hps, RET_DK, RET_DV), lambda b, h: (b, h, 0, 0)),
        ],
        out_shape=[
            jax.ShapeDtypeStruct((batch * seq, RET_V_W), BF16),
            jax.ShapeDtypeStruct((batch, RET_HEADS, RET_DK, RET_DV), F32),
        ],
        compiler_params=_params(("parallel", "arbitrary")),
        name="retention",
    )(*args)


def _merge_kernel(a_ref, ro_ref, rg_ref, gm_ref, gr_ref, x_ref, wm_ref, wr_ref, wo_ref, o_ref):
    a = jnp.dot(a_ref[...], wm_ref[...], preferred_element_type=F32)
    r = jnp.dot(rg_ref[...] * ro_ref[...], wr_ref[...], preferred_element_type=F32)
    mix = (gm_ref[...].astype(F32) * a + gr_ref[...].astype(F32) * r).astype(BF16)
    o_ref[...] = x_ref[...] + jnp.dot(mix, wo_ref[...], preferred_element_type=F32)


def _merge(a, ro, rg, gates, x, wm, wr, wo, *, tm):
    m = x.shape[0]
    row = lambda i: (i, 0)
    return pl.pallas_call(
        _merge_kernel,
        grid=(m // tm,),
        in_specs=[
            pl.BlockSpec((tm, MLA_HEADS * V_HEAD), row),
            pl.BlockSpec((tm, RET_V_W), row),
            pl.BlockSpec((tm, RET_V_W), row),
            pl.BlockSpec((tm, D_MODEL), lambda i: (i, 0)),
            pl.BlockSpec((tm, D_MODEL), lambda i: (i, 1)),
            pl.BlockSpec((tm, D_MODEL), row),
            _resident(wm.shape),
            _resident(wr.shape),
            _resident(wo.shape),
        ],
        out_specs=pl.BlockSpec((tm, D_MODEL), row),
        out_shape=jax.ShapeDtypeStruct((m, D_MODEL), F32),
        compiler_params=_params(("parallel",)),
        name="merge",
    )(a, ro, rg, gates, gates, x, wm, wr, wo)


def _rope_tables(pos, d):
    inv = ROPE_THETA ** (-jnp.arange(0, d, 2, dtype=F32) / d)
    ang = pos.astype(F32)[:, None] * inv[None, :]
    cos, sin = jnp.cos(ang), jnp.sin(ang)
    return jnp.concatenate([cos, cos], axis=-1), jnp.concatenate([-sin, sin], axis=-1)


def _half_swap(w):
    d = w.shape[-1]
    return jnp.concatenate([w[..., d // 2:], w[..., :d // 2]], axis=-1)


def _layer_weights(l, w_in, w_uq, w_uk, w_uv, w_mla_out, w_ret_out, w_out):
    wi = w_in[l]
    w_kr = wi[:, Q_LORA + KV_LORA:MLA_IN_W]
    wa = jnp.concatenate([wi[:, :MLA_IN_W], _half_swap(w_kr)], axis=1).astype(BF16)
    wq = w_uq[l].reshape(Q_LORA, MLA_HEADS, MLA_QK)
    wq_r = wq[:, :, QK_NOPE:]
    wuq = jnp.concatenate([
        wq[:, :, :QK_NOPE].reshape(Q_LORA, -1),
        wq_r.reshape(Q_LORA, -1),
        _half_swap(wq_r).reshape(Q_LORA, -1)], axis=1).astype(BF16)
    wuk = w_uk[l].reshape(KV_LORA, -1).astype(BF16)
    wuv = w_uv[l].reshape(KV_LORA, -1).astype(BF16)
    return dict(
        wa=wa, wrest=wi[:, MLA_IN_W:].astype(BF16), wuq=wuq, wuqt=wuq.T, wuk=wuk, wuvt=wuv.T,
        wukt=jnp.transpose(w_uk[l], (1, 2, 0)).astype(BF16),
        wuv=jnp.transpose(w_uv[l], (1, 0, 2)).astype(BF16),
        w_mla_out=w_mla_out[l].astype(BF16), w_ret_out=w_ret_out[l].astype(BF16), w_out=w_out[l].astype(BF16),
    )


def _tables(pos, transposed):
    c64, s64 = _rope_tables(pos, QK_ROPE)
    c128, s128 = _rope_tables(pos, RET_DK)
    tab = dict(krt=jnp.concatenate([c64, s64], axis=1), c128=c128, s128=s128)
    if transposed:
        tab.update(cq=c64.T, sq=s64.T)
    else:
        tab.update(cq=jnp.concatenate([c64, c64], axis=1), sq=jnp.concatenate([s64, s64], axis=1))
    return tab


def _mixer(h, w, g_mix, g_q, g_kv, tab, *, tm_proj, tm_mm, transposed):
    wuq = w["wuqt"] if transposed else w["wuq"]
    outs = _mla_proj(h, g_mix, w["wa"], g_q, g_kv, wuq, w["wuk"], w["wuvt"], tab["krt"], tab["cq"], tab["sq"],
                     tm=tm_proj, transposed=transposed)
    u = outs[0]
    tn = 2048
    wr = w["wrest"]
    qk = _mm(u, wr, 0, 2 * RET_QK_W, "rope", (tab["c128"], tab["s128"]), tm=tm_mm, tn=tn)
    v = _mm(u, wr, 2 * RET_QK_W, RET_V_W, "none", tm=tm_mm, tn=tn)
    rg = _mm(u, wr, 2 * RET_QK_W + RET_V_W, RET_V_W, "silu", tm=tm_mm, tn=tn)
    gates = _mm(u, wr, 2 * RET_QK_W + 2 * RET_V_W, 2 * D_MODEL, "sigmoid", tm=tm_mm, tn=tn)
    return outs[1:], qk, v, rg, gates


def kernel(x_prompt, x_sample, cache_ckv, cache_krope, state_ret, ffn1_norm, ffn1_w13, ffn1_w2, mix_norm, w_in,
           q_norm, kv_norm, w_uq, w_uk, w_uv, w_mla_out, w_ret_out, w_out, ffn2_norm, ffn2_w13, ffn2_w2,
           final_norm):
    batch, seq, _ = x_prompt.shape
    dbatch, dseq, _ = x_sample.shape
    past = cache_ckv.shape[2]
    mp, ms = batch * seq, dbatch * dseq

    tab_p = _tables(jnp.arange(seq), True)
    tab_s = _tables(jnp.tile(past + jnp.arange(dseq), dbatch), False)
    ret_tab_p = _ret_tables(256)
    ret_tab_s = _ret_tables(dseq)
    cache_krope_t = jnp.swapaxes(cache_krope, 2, 3)

    hp = x_prompt.reshape(mp, D_MODEL)
    hs = x_sample.reshape(ms, D_MODEL)
    outs = {k: [] for k in ("ckv_p", "kr_p", "ret_p", "ckv_s", "kr_s", "ret_s")}
    row = lambda v: v.reshape(1, -1)

    gf = row(final_norm)
    for l in range(DEPTH):
        w = _layer_weights(l, w_in, w_uq, w_uk, w_uv, w_mla_out, w_ret_out, w_out)
        g1, gm, gq, gkv, g2 = (row(ffn1_norm[l]), row(mix_norm[l]), row(q_norm[l]), row(kv_norm[l]),
                               row(ffn2_norm[l]))
        g_last = gf if l == DEPTH - 1 else None

        hs, w1, w3, w2 = _ffn(hs, g1, ffn1_w13, ffn1_w13, ffn1_w2, tm=ms, tf=512, cast_layer=l)
        hp = _ffn(hp, g1, w1, w3, w2, tm=1024, tf=512)

        (q, ckv, kr), qk, v, rg, gates = _mixer(hs, w, gm, gq, gkv, tab_s,
                                                tm_proj=ms, tm_mm=ms, transposed=False)
        a = _attn_sample(q, ckv, kr, cache_ckv, cache_krope_t, l, w["wukt"], w["wuv"], batch=dbatch, tq=dseq)
        ro, s_new = _retention(qk, v, ret_tab_s, state_ret, l, batch=dbatch, seq=dseq, cs=dseq, hps=RET_HEADS)
        hs = _merge(a, ro, rg, gates, hs, w["w_mla_out"], w["w_ret_out"], w["w_out"], tm=ms)
        outs["ckv_s"].append(ckv.reshape(dbatch, dseq, KV_LORA))
        outs["kr_s"].append(kr.reshape(dbatch, dseq, QK_ROPE))
        outs["ret_s"].append(s_new)

        (qt, ckv, kr, k, vt), qk, v, rg, gates = _mixer(hp, w, gm, gq, gkv, tab_p,
                                                        tm_proj=512, tm_mm=1024, transposed=True)
        a = _attn(qt, k, vt, batch=batch, seq=seq, tq=512, tk=256)
        ro, s_new = _retention(qk, v, ret_tab_p, None, l, batch=batch, seq=seq, cs=256, hps=1)
        hp = _merge(a, ro, rg, gates, hp, w["w_mla_out"], w["w_ret_out"], w["w_out"], tm=256)
        outs["ckv_p"].append(ckv.reshape(batch, seq, KV_LORA))
        outs["kr_p"].append(kr.reshape(batch, seq, QK_ROPE))
        outs["ret_p"].append(s_new)

        hs, w1, w3, w2 = _ffn(hs, g2, ffn2_w13, ffn2_w13, ffn2_w2, g_last, tm=ms, tf=512, cast_layer=l)
        hp = _ffn(hp, g2, w1, w3, w2, g_last, tm=1024, tf=512)

    y_prompt = hp.reshape(batch, seq, D_MODEL)
    y_sample = hs.reshape(dbatch, dseq, D_MODEL)
    return (y_prompt, y_sample, jnp.stack(outs["ckv_p"]), jnp.stack(outs["kr_p"]), jnp.stack(outs["ret_p"]),
            jnp.stack(outs["ckv_s"]), jnp.stack(outs["kr_s"]), jnp.stack(outs["ret_s"]))
```

```python
import functools
import math

import jax
import jax.numpy as jnp
from jax import lax
from jax.experimental import pallas as pl
from jax.experimental.pallas import tpu as pltpu

D_MODEL = 2048
DEPTH = 2
CHUNK = 64
MLA_HEADS = 8
QK_NOPE = 128
QK_ROPE = 64
V_HEAD = 128
Q_LORA = 512
KV_LORA = 512
MLA_QK = QK_NOPE + QK_ROPE
MLA_SCALE = MLA_QK ** -0.5
RET_HEADS = 8
RET_DK = 128
RET_DV = 256
RET_QK_W = RET_HEADS * RET_DK
RET_V_W = RET_HEADS * RET_DV
RET_K_SCALE = RET_DK ** -0.5
D_FF = 5632
ROPE_THETA = 10000.0
NORM_EPS = 1e-6
GN_EPS = 1e-5

MLA_IN_W = Q_LORA + KV_LORA + QK_ROPE
NOPE_W = MLA_HEADS * QK_NOPE
ROPE_W = MLA_HEADS * QK_ROPE
Q_SCALE = MLA_SCALE * math.log2(math.e)

LANE = 128
VMEM_LIMIT = 56 * 1024 * 1024

BF16 = jnp.bfloat16
F32 = jnp.float32
_NT = (((1,), (1,)), ((), ()))
_TN = (((0,), (0,)), ((), ()))


def _params(sem):
    return pltpu.CompilerParams(dimension_semantics=sem, vmem_limit_bytes=VMEM_LIMIT)


def _rms(x, g):
    return x * lax.rsqrt(jnp.mean(x * x, axis=-1, keepdims=True) + NORM_EPS) * g


def _resident(shape):
    nd = len(shape)
    return pl.BlockSpec(shape, lambda *_: (0,) * nd, pipeline_mode=pl.Buffered(1))


def _ffn_kernel(*refs, cast, final):
    n_in = 6 if final else 5
    x_ref, g_ref, w1_ref, w3_ref, w2_ref = refs[:5]
    o_ref = refs[n_in]
    xn_ref = refs[-1]
    j = pl.program_id(1)

    def swiglu(xn):
        w1, w3, w2 = w1_ref[...], w3_ref[...], w2_ref[...]
        if cast:
            w1o_ref, w3o_ref, w2o_ref = refs[n_in + 1:n_in + 4]
            w1, w3, w2 = w1.astype(BF16), w3.astype(BF16), (0.5 * w2).astype(BF16)
            w1o_ref[...] = w1
            w3o_ref[...] = w3
            w2o_ref[...] = w2
        a = jnp.dot(xn, w1, preferred_element_type=F32)
        b = jnp.dot(xn, w3, preferred_element_type=F32)
        h = (a * jax.nn.sigmoid(a) * b).astype(BF16)
        return jnp.dot(h, w2, preferred_element_type=F32)

    @pl.when(j == 0)
    def _():
        x = x_ref[...]
        xn = _rms(x, g_ref[...]).astype(BF16)
        xn_ref[...] = xn
        o_ref[...] = x + swiglu(xn)

    @pl.when(j > 0)
    def _():
        o_ref[...] += swiglu(xn_ref[...])

    if final:
        @pl.when(j == pl.num_programs(1) - 1)
        def _():
            o_ref[...] = _rms(o_ref[...], refs[5][...])


def _ffn(x, g, w1, w3, w2, gf=None, *, tm, tf, cast_layer=None):
    m = x.shape[0]
    nf = D_FF // tf
    final = gf is not None
    cast = cast_layer is not None
    vec = pl.BlockSpec((1, D_MODEL), lambda i, j: (0, 0))
    w_up = lambda off: pl.BlockSpec((D_MODEL, tf), lambda i, j: (0, j + off))
    w_down = pl.BlockSpec((tf, D_MODEL), lambda i, j: (j, 0))
    if cast:
        w_up_in = lambda off: pl.BlockSpec((None, D_MODEL, tf), lambda i, j: (cast_layer, 0, j + off))
        w_in_specs = [w_up_in(0), w_up_in(nf), pl.BlockSpec((None, tf, D_MODEL), lambda i, j: (cast_layer, j, 0))]
    else:
        w_in_specs = [w_up(0), w_up(0), w_down]
    in_specs = [pl.BlockSpec((tm, D_MODEL), lambda i, j: (i, 0)), vec] + w_in_specs
    args = [x, g, w1, w3, w2]
    if final:
        in_specs.append(vec)
        args.append(gf)
    out_specs = [pl.BlockSpec((tm, D_MODEL), lambda i, j: (i, 0))]
    out_shape = [jax.ShapeDtypeStruct((m, D_MODEL), F32)]
    if cast:
        assert m == tm, "weights must stream through exactly once when their copies are emitted"
        out_specs += [w_up(0), w_up(0), w_down]
        out_shape += [jax.ShapeDtypeStruct((D_MODEL, D_FF), BF16), jax.ShapeDtypeStruct((D_MODEL, D_FF), BF16),
                      jax.ShapeDtypeStruct((D_FF, D_MODEL), BF16)]
    outs = pl.pallas_call(
        functools.partial(_ffn_kernel, cast=cast, final=final),
        grid=(m // tm, nf),
        in_specs=in_specs,
        out_specs=out_specs,
        out_shape=out_shape,
        scratch_shapes=[pltpu.VMEM((tm, D_MODEL), BF16)],
        compiler_params=_params(("parallel", "arbitrary")),
        name="ffn_cast" if cast else "ffn",
    )(*args)
    return outs if cast else outs[0]


def _mla_proj_kernel(x_ref, g_ref, wa_ref, qg_ref, kvg_ref, wuq_ref, wuk_ref, wuvt_ref, krt_ref, cq_ref, sq_ref,
                     u_out, q_out, ckv_out, kr_out, *kv_outs, transposed):
    u = _rms(x_ref[...], g_ref[...]).astype(BF16)
    u_out[...] = u
    t = jnp.dot(u, wa_ref[...], preferred_element_type=F32)
    qn = _rms(t[:, :Q_LORA], qg_ref[...])
    c = _rms(t[:, Q_LORA:Q_LORA + KV_LORA], kvg_ref[...])
    ckv_out[...] = c
    z = t[:, Q_LORA + KV_LORA:] * krt_ref[...]
    kr = (z + pltpu.roll(z, QK_ROPE, 1))[:, :QK_ROPE]
    kr_out[...] = kr
    cq = cq_ref[...]
    sq = sq_ref[...]
    if transposed:
        tq = jnp.dot(wuq_ref[...], qn.T.astype(BF16), preferred_element_type=F32)
        for h in range(MLA_HEADS):
            q_out[h, :QK_NOPE, :] = (tq[h * QK_NOPE:(h + 1) * QK_NOPE] * Q_SCALE).astype(BF16)
            lo = NOPE_W + h * QK_ROPE
            qr = tq[lo:lo + QK_ROPE] * cq + tq[lo + ROPE_W:lo + ROPE_W + QK_ROPE] * sq
            q_out[h, QK_NOPE:, :] = (qr * Q_SCALE).astype(BF16)
        k_out, vt_out = kv_outs
        kn = jnp.dot(c.astype(BF16), wuk_ref[...], preferred_element_type=F32)
        vt = jnp.dot(wuvt_ref[...], c.T.astype(BF16), preferred_element_type=F32)
        krb = kr.astype(BF16)
        for h in range(MLA_HEADS):
            k_out[h, :, :QK_NOPE] = kn[:, h * QK_NOPE:(h + 1) * QK_NOPE].astype(BF16)
            k_out[h, :, QK_NOPE:] = krb
            vt_out[h] = vt[h * V_HEAD:(h + 1) * V_HEAD].astype(BF16)
    else:
        tq = jnp.dot(qn.astype(BF16), wuq_ref[...], preferred_element_type=F32)
        for pair in range(MLA_HEADS // 2):
            lo = NOPE_W + pair * LANE
            qr = (tq[:, lo:lo + LANE] * cq + tq[:, lo + ROPE_W:lo + ROPE_W + LANE] * sq) * Q_SCALE
            for e in range(2):
                h = 2 * pair + e
                q_out[h, :, :QK_NOPE] = (tq[:, h * QK_NOPE:(h + 1) * QK_NOPE] * Q_SCALE).astype(BF16)
                q_out[h, :, QK_NOPE:] = qr[:, e * QK_ROPE:(e + 1) * QK_ROPE].astype(BF16)


def _mla_proj(x, g, wa, qg, kvg, wuq, wuk, wuvt, krt, cq, sq, *, tm, transposed):
    m = x.shape[0]
    row = lambda i: (i, 0)
    out_specs = [pl.BlockSpec((tm, D_MODEL), row)]
    out_shape = [jax.ShapeDtypeStruct((m, D_MODEL), BF16)]
    if transposed:
        nt = krt.shape[0] // tm
        q_tab = pl.BlockSpec((QK_ROPE, tm), lambda i: (0, i % nt))
        out_specs.append(pl.BlockSpec((MLA_HEADS, MLA_QK, tm), lambda i: (0, 0, i)))
        out_shape.append(jax.ShapeDtypeStruct((MLA_HEADS, MLA_QK, m), BF16))
    else:
        nt = krt.shape[0] // tm
        q_tab = pl.BlockSpec((tm, LANE), lambda i: (i % nt, 0))
        out_specs.append(pl.BlockSpec((MLA_HEADS, tm, MLA_QK), lambda i: (0, i, 0)))
        out_shape.append(jax.ShapeDtypeStruct((MLA_HEADS, m, MLA_QK), BF16))
    out_specs += [pl.BlockSpec((tm, KV_LORA), row), pl.BlockSpec((tm, QK_ROPE), row)]
    out_shape += [jax.ShapeDtypeStruct((m, KV_LORA), F32), jax.ShapeDtypeStruct((m, QK_ROPE), F32)]
    if transposed:
        out_specs += [pl.BlockSpec((MLA_HEADS, tm, MLA_QK), lambda i: (0, i, 0)),
                      pl.BlockSpec((MLA_HEADS, V_HEAD, tm), lambda i: (0, 0, i))]
        out_shape += [jax.ShapeDtypeStruct((MLA_HEADS, m, MLA_QK), BF16),
                      jax.ShapeDtypeStruct((MLA_HEADS, V_HEAD, m), BF16)]
    return pl.pallas_call(
        functools.partial(_mla_proj_kernel, transposed=transposed),
        grid=(m // tm,),
        in_specs=[
            pl.BlockSpec((tm, D_MODEL), row),
            _resident((1, D_MODEL)),
            _resident(wa.shape),
            _resident((1, Q_LORA)),
            _resident((1, KV_LORA)),
            _resident(wuq.shape),
            _resident(wuk.shape),
            _resident(wuvt.shape),
            pl.BlockSpec((tm, LANE), lambda i: (i % nt, 0)),
            q_tab,
            q_tab,
        ],
        out_specs=out_specs,
        out_shape=out_shape,
        compiler_params=_params(("parallel",)),
        name="mla_proj",
    )(x, g, wa, qg, kvg, wuq, wuk, wuvt, krt, cq, sq)


MM_SUB = 512


def _mm_kernel(*refs, epilogue, tn):
    if epilogue == "rope":
        u_ref, w_ref, c_ref, s_ref, o_ref = refs
    else:
        u_ref, w_ref, o_ref = refs
    u = u_ref[...]
    for n in range(tn // MM_SUB):
        cols = slice(n * MM_SUB, (n + 1) * MM_SUB)
        acc = jnp.dot(u, w_ref[:, cols], preferred_element_type=F32)
        if epilogue == "rope":
            is_k = pl.program_id(1) * tn + n * MM_SUB >= RET_QK_W
            scale = jnp.where(is_k, RET_K_SCALE, 1.0).astype(F32)
            c = c_ref[...] * scale
            s = s_ref[...] * scale
            for h in range(MM_SUB // RET_DK):
                xh = acc[:, h * RET_DK:(h + 1) * RET_DK]
                lo = n * MM_SUB + h * RET_DK
                o_ref[:, lo:lo + RET_DK] = (xh * c + pltpu.roll(xh, RET_DK // 2, 1) * s).astype(BF16)
        elif epilogue == "silu":
            o_ref[:, cols] = (acc * jax.nn.sigmoid(acc)).astype(BF16)
        elif epilogue == "sigmoid":
            o_ref[:, cols] = jax.nn.sigmoid(acc).astype(BF16)
        else:
            o_ref[:, cols] = acc.astype(BF16)


def _mm(u, w, col0, ncols, epilogue, tables=None, *, tm, tn):
    m = u.shape[0]
    c0 = col0 // tn
    in_specs = [
        pl.BlockSpec((tm, D_MODEL), lambda i, j: (i, 0)),
        pl.BlockSpec((D_MODEL, tn), lambda i, j: (0, c0 + j)),
    ]
    args = [u, w]
    if epilogue == "rope":
        nt = tables[0].shape[0] // tm
        in_specs += [pl.BlockSpec((tm, LANE), lambda i, j: (i % nt, 0))] * 2
        args += list(tables)
    return pl.pallas_call(
        functools.partial(_mm_kernel, epilogue=epilogue, tn=tn),
        grid=(m // tm, ncols // tn),
        in_specs=in_specs,
        out_specs=pl.BlockSpec((tm, tn), lambda i, j: (i, j)),
        out_shape=jax.ShapeDtypeStruct((m, ncols), BF16),
        compiler_params=_params(("parallel", "arbitrary")),
        name="mm_" + epilogue,
    )(*args)


def _attn_kernel(qt_ref, k_ref, vt_ref, bias_ref, o_ref, m_ref, l_ref, acc_ref, s_ref, p_ref, a_ref, *, tq, tk, nq):
    assert tq == 2 * tk
    for i in range(nq):
        st = i % 2
        qt = qt_ref[:, i * tq:(i + 1) * tq]
        m_ref[st] = jnp.full((1, tq), -jnp.inf, F32)
        l_ref[st] = jnp.zeros((1, tq), F32)
        acc_ref[st] = jnp.zeros((V_HEAD, tq), F32)
        p_ref[st, 1] = jnp.zeros((tk, tq), BF16)
        a_ref[st, 1] = jnp.ones((1, tq), F32)

        def scores(j, slot):
            s_ref[st, slot] = jnp.dot(k_ref[j * tk:(j + 1) * tk, :], qt, preferred_element_type=F32)

        def softmax(slot, bias):
            s = s_ref[st, slot]
            if bias is not None:
                s = s + bias
            m = m_ref[st]
            m_new = jnp.maximum(m, jnp.max(s, axis=0, keepdims=True))
            alpha = jnp.exp2(m - m_new)
            p = jnp.exp2(s - m_new)
            l_ref[st] = alpha * l_ref[st] + jnp.sum(p, axis=0, keepdims=True)
            m_ref[st] = m_new
            p_ref[st, slot] = p.astype(BF16)
            a_ref[st, slot] = alpha

        def pv(j, slot):
            acc_ref[st] = a_ref[st, slot] * acc_ref[st] + jnp.dot(
                vt_ref[:, j * tk:(j + 1) * tk], p_ref[st, slot], preferred_element_type=F32)

        scores(0, 0)
        for t in range(i):
            scores(2 * t + 1, 1)
            softmax(0, None)
            pv(max(2 * t - 1, 0), 1)
            scores(2 * t + 2, 0)
            softmax(1, None)
            pv(2 * t, 0)
        scores(2 * i + 1, 1)
        softmax(0, bias_ref[0])
        pv(max(2 * i - 1, 0), 1)
        softmax(1, bias_ref[1])
        pv(2 * i, 0)
        pv(2 * i + 1, 1)
        o_ref[i * tq:(i + 1) * tq, :] = (acc_ref[st] * (1.0 / l_ref[st])).T.astype(BF16)


def _attn_bias(tq, tk):
    key_chunk = jnp.arange(tq)[:, None] // CHUNK
    query_chunk = jnp.arange(tq)[None, :] // CHUNK
    bias = jnp.where(key_chunk <= query_chunk, 0.0, -jnp.inf).astype(F32)
    return bias.reshape(tq // tk, tk, tq)


def _attn(qt, k, vt, *, batch, seq, tq, tk):
    nq = seq // tq
    bias = _attn_bias(tq, tk)
    return pl.pallas_call(
        functools.partial(_attn_kernel, tq=tq, tk=tk, nq=nq),
        grid=(batch, MLA_HEADS),
        in_specs=[
            pl.BlockSpec((None, MLA_QK, seq), lambda b, h: (h, 0, b)),
            pl.BlockSpec((None, seq, MLA_QK), lambda b, h: (h, b, 0)),
            pl.BlockSpec((None, V_HEAD, seq), lambda b, h: (h, 0, b)),
            _resident(bias.shape),
        ],
        out_specs=pl.BlockSpec((seq, V_HEAD), lambda b, h: (b, h)),
        out_shape=jax.ShapeDtypeStruct((batch * seq, MLA_HEADS * V_HEAD), BF16),
        scratch_shapes=[pltpu.VMEM((2, 1, tq), F32), pltpu.VMEM((2, 1, tq), F32), pltpu.VMEM((2, V_HEAD, tq), F32),
                        pltpu.VMEM((2, 2, tk, tq), F32), pltpu.VMEM((2, 2, tk, tq), BF16),
                        pltpu.VMEM((2, 2, 1, tq), F32)],
        compiler_params=_params(("parallel", "arbitrary")),
        name="attn_prompt",
    )(qt, k, vt, bias)


def _attn_sample_kernel(q_ref, cn_ref, krn_ref, cc_ref, krc_ref, wukt_ref, wuv_ref, o_ref, ql_ref, qr_ref,
                        *, tq):
    for h in range(MLA_HEADS):
        qh = q_ref[h]
        ql_ref[h * tq:(h + 1) * tq, :] = jnp.dot(
            qh[:, :QK_NOPE], wukt_ref[h], preferred_element_type=F32).astype(BF16)
        qr_ref[h * tq:(h + 1) * tq, :] = qh[:, QK_NOPE:]
    ql = ql_ref[...]
    qr = qr_ref[...]
    cc = cc_ref[...].astype(BF16)
    krc_t = krc_ref[...].astype(BF16)
    cn = cn_ref[...].astype(BF16)
    krn = krn_ref[...].astype(BF16)
    s_c = (lax.dot_general(ql, cc, _NT, preferred_element_type=F32)
           + jnp.dot(qr, krc_t, preferred_element_type=F32))
    s_n = (lax.dot_general(ql, cn, _NT, preferred_element_type=F32)
           + lax.dot_general(qr, krn, _NT, preferred_element_type=F32))
    m = jnp.maximum(jnp.max(s_c, axis=-1, keepdims=True), jnp.max(s_n, axis=-1, keepdims=True))
    p_c = jnp.exp2(s_c - m)
    p_n = jnp.exp2(s_n - m)
    l = jnp.sum(p_c, axis=-1, keepdims=True) + jnp.sum(p_n, axis=-1, keepdims=True)
    ol = (jnp.dot(p_c.astype(BF16), cc, preferred_element_type=F32)
          + jnp.dot(p_n.astype(BF16), cn, preferred_element_type=F32)) / l
    ol = ol.astype(BF16)
    for h in range(MLA_HEADS):
        o_ref[:, h * V_HEAD:(h + 1) * V_HEAD] = jnp.dot(
            ol[h * tq:(h + 1) * tq], wuv_ref[h], preferred_element_type=F32).astype(BF16)


def _attn_sample(q, c_new, kr_new, cache_c, cache_kr, layer, wukt, wuv, *, batch, tq):
    past = cache_c.shape[2]
    return pl.pallas_call(
        functools.partial(_attn_sample_kernel, tq=tq),
        grid=(batch,),
        in_specs=[
            pl.BlockSpec((MLA_HEADS, tq, MLA_QK), lambda b: (0, b, 0)),
            pl.BlockSpec((tq, KV_LORA), lambda b: (b, 0)),
            pl.BlockSpec((tq, QK_ROPE), lambda b: (b, 0)),
            pl.BlockSpec((None, None, past, KV_LORA), lambda b: (layer, b, 0, 0)),
            pl.BlockSpec((None, None, QK_ROPE, past), lambda b: (layer, b, 0, 0)),
            _resident(wukt.shape),
            _resident(wuv.shape),
        ],
        out_specs=pl.BlockSpec((tq, MLA_HEADS * V_HEAD), lambda b: (b, 0)),
        out_shape=jax.ShapeDtypeStruct((batch * tq, MLA_HEADS * V_HEAD), BF16),
        scratch_shapes=[pltpu.VMEM((MLA_HEADS * tq, KV_LORA), BF16), pltpu.VMEM((MLA_HEADS * tq, QK_ROPE), BF16)],
        compiler_params=_params(("parallel",)),
        name="attn_sample",
    )(q, c_new, kr_new, cache_c, cache_kr, wukt, wuv)


def _ret_kernel(*refs, seq, cs, hps, has_state):
    if has_state:
        q_ref, k_ref, v_ref, dm_ref, qd_ref, kd_ref, cd_ref, s0_ref, o_ref, s_out = refs
    else:
        q_ref, k_ref, v_ref, dm_ref, qd_ref, kd_ref, cd_ref, o_ref, s_out = refs
    states = [s0_ref[h] if has_state else jnp.zeros((RET_DK, RET_DV), F32) for h in range(hps)]
    for c in range(seq // cs):
        rows = slice(c * cs, (c + 1) * cs)
        for h in range(hps):
            kcol = slice(h * RET_DK, (h + 1) * RET_DK)
            vcol = slice(h * RET_DV, (h + 1) * RET_DV)
            qc = q_ref[rows, kcol]
            kc = k_ref[rows, kcol]
            vc = v_ref[rows, vcol]
            att = lax.dot_general(qc, kc, _NT, preferred_element_type=F32) * dm_ref[h]
            o = (jnp.dot(att.astype(BF16), vc, preferred_element_type=F32)
                 + jnp.dot(qc, states[h].astype(BF16), preferred_element_type=F32) * qd_ref[h])
            kdec = (kc.astype(F32) * kd_ref[h]).astype(BF16)
            states[h] = states[h] * cd_ref[h] + lax.dot_general(kdec, vc, _TN, preferred_element_type=F32)
            mu = jnp.mean(o, axis=-1, keepdims=True)
            d = o - mu
            var = jnp.mean(d * d, axis=-1, keepdims=True)
            o_ref[rows, vcol] = (d * lax.rsqrt(var + GN_EPS)).astype(BF16)
    for h in range(hps):
        s_out[h] = states[h]


def _ret_tables(cs):
    lg = jnp.log1p(-jnp.exp2(-5.0 - jnp.arange(RET_HEADS, dtype=F32)))
    j = jnp.arange(cs, dtype=F32)
    diff = j[:, None] - j[None, :]
    dmask = jnp.where(diff[None] >= 0, jnp.exp(jnp.maximum(diff, 0.0)[None] * lg[:, None, None]), 0.0)
    q_dec = jnp.exp((j + 1.0)[None] * lg[:, None])[..., None]
    k_dec = jnp.exp((cs - 1.0 - j)[None] * lg[:, None])[..., None]
    c_dec = jnp.exp(cs * lg)[:, None, None]
    return (dmask,
            jnp.broadcast_to(q_dec, (RET_HEADS, cs, RET_DV)),
            jnp.broadcast_to(k_dec, (RET_HEADS, cs, RET_DK)),
            jnp.broadcast_to(c_dec, (RET_HEADS, 1, RET_DV)))


def _retention(qk, v, tables, state, layer, *, batch, seq, cs, hps):
    dmask, q_dec, k_dec, c_dec = tables
    k_col = RET_HEADS // hps
    head = lambda b, h: (h, 0, 0)
    in_specs = [
        pl.BlockSpec((seq, hps * RET_DK), lambda b, h: (b, h)),
        pl.BlockSpec((seq, hps * RET_DK), lambda b, h: (b, k_col + h)),
        pl.BlockSpec((seq, hps * RET_DV), lambda b, h: (b, h)),
        pl.BlockSpec((hps, cs, cs), head),
        pl.BlockSpec((hps, cs, RET_DV), head),
        pl.BlockSpec((hps, cs, RET_DK), head),
        pl.BlockSpec((hps, 1, RET_DV), head),
    ]
    args = [qk, qk, v, dmask, q_dec, k_dec, c_dec]
    if state is not None:
        in_specs.append(pl.BlockSpec((None, None, hps, RET_DK, RET_DV), lambda b, h: (layer, b, h, 0, 0)))
        args.append(state)
    return pl.pallas_call(
        functools.partial(_ret_kernel, seq=seq, cs=cs, hps=hps, has_state=state is not None),
        grid=(batch, RET_HEADS // hps),
        in_specs=in_specs,
        out_specs=[
            pl.BlockSpec((seq, hps * RET_DV), lambda b, h: (b, h)),
            pl.BlockSpec((None, hps, RET_DK, RET_DV), lambda b, h: (b, h, 0, 0)),
        ],
        out_shape=[
            jax.ShapeDtypeStruct((batch * seq, RET_V_W), BF16),
            jax.ShapeDtypeStruct((batch, RET_HEADS, RET_DK, RET_DV), F32),
        ],
        compiler_params=_params(("parallel", "arbitrary")),
        name="retention",
    )(*args)


def _merge_kernel(a_ref, ro_ref, rg_ref, gm_ref, gr_ref, x_ref, wm_ref, wr_ref, wo_ref, o_ref):
    a = jnp.dot(a_ref[...], wm_ref[...], preferred_element_type=F32)
    r = jnp.dot(rg_ref[...] * ro_ref[...], wr_ref[...], preferred_element_type=F32)
    mix = (gm_ref[...].astype(F32) * a + gr_ref[...].astype(F32) * r).astype(BF16)
    o_ref[...] = x_ref[...] + jnp.dot(mix, wo_ref[...], preferred_element_type=F32)


def _merge(a, ro, rg, gates, x, wm, wr, wo, *, tm):
    m = x.shape[0]
    row = lambda i: (i, 0)
    return pl.pallas_call(
        _merge_kernel,
        grid=(m // tm,),
        in_specs=[
            pl.BlockSpec((tm, MLA_HEADS * V_HEAD), row),
            pl.BlockSpec((tm, RET_V_W), row),
            pl.BlockSpec((tm, RET_V_W), row),
            pl.BlockSpec((tm, D_MODEL), lambda i: (i, 0)),
            pl.BlockSpec((tm, D_MODEL), lambda i: (i, 1)),
            pl.BlockSpec((tm, D_MODEL), row),
            _resident(wm.shape),
            _resident(wr.shape),
            _resident(wo.shape),
        ],
        out_specs=pl.BlockSpec((tm, D_MODEL), row),
        out_shape=jax.ShapeDtypeStruct((m, D_MODEL), F32),
        compiler_params=_params(("parallel",)),
        name="merge",
    )(a, ro, rg, gates, gates, x, wm, wr, wo)


def _rope_tables(pos, d):
    inv = ROPE_THETA ** (-jnp.arange(0, d, 2, dtype=F32) / d)
    ang = pos.astype(F32)[:, None] * inv[None, :]
    cos, sin = jnp.cos(ang), jnp.sin(ang)
    return jnp.concatenate([cos, cos], axis=-1), jnp.concatenate([-sin, sin], axis=-1)


def _half_swap(w):
    d = w.shape[-1]
    return jnp.concatenate([w[..., d // 2:], w[..., :d // 2]], axis=-1)


def _layer_weights(l, w_in, w_uq, w_uk, w_uv, w_mla_out, w_ret_out, w_out):
    wi = w_in[l]
    w_kr = wi[:, Q_LORA + KV_LORA:MLA_IN_W]
    wa = jnp.concatenate([wi[:, :MLA_IN_W], _half_swap(w_kr)], axis=1).astype(BF16)
    wq = w_uq[l].reshape(Q_LORA, MLA_HEADS, MLA_QK)
    wq_r = wq[:, :, QK_NOPE:]
    wuq = jnp.concatenate([
        wq[:, :, :QK_NOPE].reshape(Q_LORA, -1),
        wq_r.reshape(Q_LORA, -1),
        _half_swap(wq_r).reshape(Q_LORA, -1)], axis=1).astype(BF16)
    wuk = w_uk[l].reshape(KV_LORA, -1).astype(BF16)
    wuv = w_uv[l].reshape(KV_LORA, -1).astype(BF16)
    return dict(
        wa=wa, wrest=wi[:, MLA_IN_W:].astype(BF16), wuq=wuq, wuqt=wuq.T, wuk=wuk, wuvt=wuv.T,
        wukt=jnp.transpose(w_uk[l], (1, 2, 0)).astype(BF16),
        wuv=jnp.transpose(w_uv[l], (1, 0, 2)).astype(BF16),
        w_mla_out=w_mla_out[l].astype(BF16), w_ret_out=w_ret_out[l].astype(BF16), w_out=w_out[l].astype(BF16),
    )


def _tables(pos, transposed):
    c64, s64 = _rope_tables(pos, QK_ROPE)
    c128, s128 = _rope_tables(pos, RET_DK)
    tab = dict(krt=jnp.concatenate([c64, s64], axis=1), c128=c128, s128=s128)
    if transposed:
        tab.update(cq=c64.T, sq=s64.T)
    else:
        tab.update(cq=jnp.concatenate([c64, c64], axis=1), sq=jnp.concatenate([s64, s64], axis=1))
    return tab


def _mixer(h, w, g_mix, g_q, g_kv, tab, *, tm_proj, tm_mm, transposed):
    wuq = w["wuqt"] if transposed else w["wuq"]
    outs = _mla_proj(h, g_mix, w["wa"], g_q, g_kv, wuq, w["wuk"], w["wuvt"], tab["krt"], tab["cq"], tab["sq"],
                     tm=tm_proj, transposed=transposed)
    u = outs[0]
    tn = 2048
    wr = w["wrest"]
    qk = _mm(u, wr, 0, 2 * RET_QK_W, "rope", (tab["c128"], tab["s128"]), tm=tm_mm, tn=tn)
    v = _mm(u, wr, 2 * RET_QK_W, RET_V_W, "none", tm=tm_mm, tn=tn)
    rg = _mm(u, wr, 2 * RET_QK_W + RET_V_W, RET_V_W, "silu", tm=tm_mm, tn=tn)
    gates = _mm(u, wr, 2 * RET_QK_W + 2 * RET_V_W, 2 * D_MODEL, "sigmoid", tm=tm_mm, tn=tn)
    return outs[1:], qk, v, rg, gates


def kernel(x_prompt, x_sample, cache_ckv, cache_krope, state_ret, ffn1_norm, ffn1_w13, ffn1_w2, mix_norm, w_in,
           q_norm, kv_norm, w_uq, w_uk, w_uv, w_mla_out, w_ret_out, w_out, ffn2_norm, ffn2_w13, ffn2_w2,
           final_norm):
    batch, seq, _ = x_prompt.shape
    dbatch, dseq, _ = x_sample.shape
    past = cache_ckv.shape[2]
    mp, ms = batch * seq, dbatch * dseq

    tab_p = _tables(jnp.arange(seq), True)
    tab_s = _tables(jnp.tile(past + jnp.arange(dseq), dbatch), False)
    ret_tab_p = _ret_tables(256)
    ret_tab_s = _ret_tables(dseq)
    cache_krope_t = jnp.swapaxes(cache_krope, 2, 3)

    hp = x_prompt.reshape(mp, D_MODEL)
    hs = x_sample.reshape(ms, D_MODEL)
    outs = {k: [] for k in ("ckv_p", "kr_p", "ret_p", "ckv_s", "kr_s", "ret_s")}
    row = lambda v: v.reshape(1, -1)

    gf = row(final_norm)
    for l in range(DEPTH):
        w = _layer_weights(l, w_in, w_uq, w_uk, w_uv, w_mla_out, w_ret_out, w_out)
        g1, gm, gq, gkv, g2 = (row(ffn1_norm[l]), row(mix_norm[l]), row(q_norm[l]), row(kv_norm[l]),
                               row(ffn2_norm[l]))
        g_last = gf if l == DEPTH - 1 else None

        hs, w1, w3, w2 = _ffn(hs, g1, ffn1_w13, ffn1_w13, ffn1_w2, tm=ms, tf=512, cast_layer=l)
        hp = _ffn(hp, g1, w1, w3, w2, tm=1024, tf=512)

        (q, ckv, kr), qk, v, rg, gates = _mixer(hs, w, gm, gq, gkv, tab_s,
                                                tm_proj=ms, tm_mm=ms, transposed=False)
        a = _attn_sample(q, ckv, kr, cache_ckv, cache_krope_t, l, w["wukt"], w["wuv"], batch=dbatch, tq=dseq)
        ro, s_new = _retention(qk, v, ret_tab_s, state_ret, l, batch=dbatch, seq=dseq, cs=dseq, hps=RET_HEADS)
        hs = _merge(a, ro, rg, gates, hs, w["w_mla_out"], w["w_ret_out"], w["w_out"], tm=ms)
        outs["ckv_s"].append(ckv.reshape(dbatch, dseq, KV_LORA))
        outs["kr_s"].append(kr.reshape(dbatch, dseq, QK_ROPE))
        outs["ret_s"].append(s_new)

        (qt, ckv, kr, k, vt), qk, v, rg, gates = _mixer(hp, w, gm, gq, gkv, tab_p,
                                                        tm_proj=512, tm_mm=1024, transposed=True)
        a = _attn(qt, k, vt, batch=batch, seq=seq, tq=512, tk=256)
        ro, s_new = _retention(qk, v, ret_tab_p, None, l, batch=batch, seq=seq, cs=256, hps=2)
        hp = _merge(a, ro, rg, gates, hp, w["w_mla_out"], w["w_ret_out"], w["w_out"], tm=256)
        outs["ckv_p"].append(ckv.reshape(batch, seq, KV_LORA))
        outs["kr_p"].append(kr.reshape(batch, seq, QK_ROPE))
        outs["ret_p"].append(s_new)

        hs, w1, w3, w2 = _ffn(hs, g2, ffn2_w13, ffn2_w13, ffn2_w2, g_last, tm=ms, tf=512, cast_layer=l)
        hp = _ffn(hp, g2, w1, w3, w2, g_last, tm=1024, tf=512)

    y_prompt = hp.reshape(batch, seq, D_MODEL)
    y_sample = hs.reshape(dbatch, dseq, D_MODEL)
    return (y_prompt, y_sample, jnp.stack(outs["ckv_p"]), jnp.stack(outs["kr_p"]), jnp.stack(outs["ret_p"]),
            jnp.stack(outs["ckv_s"]), jnp.stack(outs["kr_s"]), jnp.stack(outs["ret_s"]))
```

```python
import functools
import math

import jax
import jax.numpy as jnp
from jax import lax
from jax.experimental import pallas as pl
from jax.experimental.pallas import tpu as pltpu

D_MODEL = 2048
DEPTH = 2
CHUNK = 64
MLA_HEADS = 8
QK_NOPE = 128
QK_ROPE = 64
V_HEAD = 128
Q_LORA = 512
KV_LORA = 512
MLA_QK = QK_NOPE + QK_ROPE
MLA_SCALE = MLA_QK ** -0.5
RET_HEADS = 8
RET_DK = 128
RET_DV = 256
RET_QK_W = RET_HEADS * RET_DK
RET_V_W = RET_HEADS * RET_DV
RET_K_SCALE = RET_DK ** -0.5
D_FF = 5632
ROPE_THETA = 10000.0
NORM_EPS = 1e-6
GN_EPS = 1e-5

MLA_IN_W = Q_LORA + KV_LORA + QK_ROPE
NOPE_W = MLA_HEADS * QK_NOPE
ROPE_W = MLA_HEADS * QK_ROPE
Q_SCALE = MLA_SCALE * math.log2(math.e)

LANE = 128
VMEM_LIMIT = 56 * 1024 * 1024

BF16 = jnp.bfloat16
F32 = jnp.float32
_NT = (((1,), (1,)), ((), ()))
_TN = (((0,), (0,)), ((), ()))


def _params(sem):
    return pltpu.CompilerParams(dimension_semantics=sem, vmem_limit_bytes=VMEM_LIMIT)


def _rms(x, g):
    return x * lax.rsqrt(jnp.mean(x * x, axis=-1, keepdims=True) + NORM_EPS) * g


def _resident(shape):
    nd = len(shape)
    return pl.BlockSpec(shape, lambda *_: (0,) * nd, pipeline_mode=pl.Buffered(1))


def _ffn_kernel(*refs, cast, final):
    n_in = 6 if final else 5
    x_ref, g_ref, w1_ref, w3_ref, w2_ref = refs[:5]
    o_ref = refs[n_in]
    xn_ref = refs[-1]
    j = pl.program_id(1)

    def swiglu(xn):
        w1, w3, w2 = w1_ref[...], w3_ref[...], w2_ref[...]
        if cast:
            w1o_ref, w3o_ref, w2o_ref = refs[n_in + 1:n_in + 4]
            w1, w3, w2 = w1.astype(BF16), w3.astype(BF16), (0.5 * w2).astype(BF16)
            w1o_ref[...] = w1
            w3o_ref[...] = w3
            w2o_ref[...] = w2
        a = jnp.dot(xn, w1, preferred_element_type=F32)
        b = jnp.dot(xn, w3, preferred_element_type=F32)
        h = (a * jax.nn.sigmoid(a) * b).astype(BF16)
        return jnp.dot(h, w2, preferred_element_type=F32)

    @pl.when(j == 0)
    def _():
        x = x_ref[...]
        xn = _rms(x, g_ref[...]).astype(BF16)
        xn_ref[...] = xn
        o_ref[...] = x + swiglu(xn)

    @pl.when(j > 0)
    def _():
        o_ref[...] += swiglu(xn_ref[...])

    if final:
        @pl.when(j == pl.num_programs(1) - 1)
        def _():
            o_ref[...] = _rms(o_ref[...], refs[5][...])


def _ffn(x, g, w1, w3, w2, gf=None, *, tm, tf, cast_layer=None):
    m = x.shape[0]
    nf = D_FF // tf
    final = gf is not None
    cast = cast_layer is not None
    vec = pl.BlockSpec((1, D_MODEL), lambda i, j: (0, 0))
    w_up = lambda off: pl.BlockSpec((D_MODEL, tf), lambda i, j: (0, j + off))
    w_down = pl.BlockSpec((tf, D_MODEL), lambda i, j: (j, 0))
    if cast:
        w_up_in = lambda off: pl.BlockSpec((None, D_MODEL, tf), lambda i, j: (cast_layer, 0, j + off))
        w_in_specs = [w_up_in(0), w_up_in(nf), pl.BlockSpec((None, tf, D_MODEL), lambda i, j: (cast_layer, j, 0))]
    else:
        w_in_specs = [w_up(0), w_up(0), w_down]
    in_specs = [pl.BlockSpec((tm, D_MODEL), lambda i, j: (i, 0)), vec] + w_in_specs
    args = [x, g, w1, w3, w2]
    if final:
        in_specs.append(vec)
        args.append(gf)
    out_specs = [pl.BlockSpec((tm, D_MODEL), lambda i, j: (i, 0))]
    out_shape = [jax.ShapeDtypeStruct((m, D_MODEL), F32)]
    if cast:
        assert m == tm, "weights must stream through exactly once when their copies are emitted"
        out_specs += [w_up(0), w_up(0), w_down]
        out_shape += [jax.ShapeDtypeStruct((D_MODEL, D_FF), BF16), jax.ShapeDtypeStruct((D_MODEL, D_FF), BF16),
                      jax.ShapeDtypeStruct((D_FF, D_MODEL), BF16)]
    outs = pl.pallas_call(
        functools.partial(_ffn_kernel, cast=cast, final=final),
        grid=(m // tm, nf),
        in_specs=in_specs,
        out_specs=out_specs,
        out_shape=out_shape,
        scratch_shapes=[pltpu.VMEM((tm, D_MODEL), BF16)],
        compiler_params=_params(("parallel", "arbitrary")),
        name="ffn_cast" if cast else "ffn",
    )(*args)
    return outs if cast else outs[0]


def _mla_proj_kernel(x_ref, g_ref, wa_ref, qg_ref, kvg_ref, wuq_ref, wuk_ref, wuvt_ref, krt_ref, cq_ref, sq_ref,
                     u_out, q_out, ckv_out, kr_out, *kv_outs, transposed):
    u = _rms(x_ref[...], g_ref[...]).astype(BF16)
    u_out[...] = u
    t = jnp.dot(u, wa_ref[...], preferred_element_type=F32)
    qn = _rms(t[:, :Q_LORA], qg_ref[...])
    c = _rms(t[:, Q_LORA:Q_LORA + KV_LORA], kvg_ref[...])
    ckv_out[...] = c
    z = t[:, Q_LORA + KV_LORA:] * krt_ref[...]
    kr = (z + pltpu.roll(z, QK_ROPE, 1))[:, :QK_ROPE]
    kr_out[...] = kr
    cq = cq_ref[...]
    sq = sq_ref[...]
    if transposed:
        tq = jnp.dot(wuq_ref[...], qn.T.astype(BF16), preferred_element_type=F32)
        for h in range(MLA_HEADS):
            q_out[h, :QK_NOPE, :] = (tq[h * QK_NOPE:(h + 1) * QK_NOPE] * Q_SCALE).astype(BF16)
            lo = NOPE_W + h * QK_ROPE
            qr = tq[lo:lo + QK_ROPE] * cq + tq[lo + ROPE_W:lo + ROPE_W + QK_ROPE] * sq
            q_out[h, QK_NOPE:, :] = (qr * Q_SCALE).astype(BF16)
        k_out, vt_out = kv_outs
        kn = jnp.dot(c.astype(BF16), wuk_ref[...], preferred_element_type=F32)
        vt = jnp.dot(wuvt_ref[...], c.T.astype(BF16), preferred_element_type=F32)
        krb = kr.astype(BF16)
        for h in range(MLA_HEADS):
            k_out[h, :, :QK_NOPE] = kn[:, h * QK_NOPE:(h + 1) * QK_NOPE].astype(BF16)
            k_out[h, :, QK_NOPE:] = krb
            vt_out[h] = vt[h * V_HEAD:(h + 1) * V_HEAD].astype(BF16)
    else:
        tq = jnp.dot(qn.astype(BF16), wuq_ref[...], preferred_element_type=F32)
        for pair in range(MLA_HEADS // 2):
            lo = NOPE_W + pair * LANE
            qr = (tq[:, lo:lo + LANE] * cq + tq[:, lo + ROPE_W:lo + ROPE_W + LANE] * sq) * Q_SCALE
            for e in range(2):
                h = 2 * pair + e
                q_out[h, :, :QK_NOPE] = (tq[:, h * QK_NOPE:(h + 1) * QK_NOPE] * Q_SCALE).astype(BF16)
                q_out[h, :, QK_NOPE:] = qr[:, e * QK_ROPE:(e + 1) * QK_ROPE].astype(BF16)


def _mla_proj(x, g, wa, qg, kvg, wuq, wuk, wuvt, krt, cq, sq, *, tm, transposed):
    m = x.shape[0]
    row = lambda i: (i, 0)
    out_specs = [pl.BlockSpec((tm, D_MODEL), row)]
    out_shape = [jax.ShapeDtypeStruct((m, D_MODEL), BF16)]
    if transposed:
        nt = krt.shape[0] // tm
        q_tab = pl.BlockSpec((QK_ROPE, tm), lambda i: (0, i % nt))
        out_specs.append(pl.BlockSpec((MLA_HEADS, MLA_QK, tm), lambda i: (0, 0, i)))
        out_shape.append(jax.ShapeDtypeStruct((MLA_HEADS, MLA_QK, m), BF16))
    else:
        nt = krt.shape[0] // tm
        q_tab = pl.BlockSpec((tm, LANE), lambda i: (i % nt, 0))
        out_specs.append(pl.BlockSpec((MLA_HEADS, tm, MLA_QK), lambda i: (0, i, 0)))
        out_shape.append(jax.ShapeDtypeStruct((MLA_HEADS, m, MLA_QK), BF16))
    out_specs += [pl.BlockSpec((tm, KV_LORA), row), pl.BlockSpec((tm, QK_ROPE), row)]
    out_shape += [jax.ShapeDtypeStruct((m, KV_LORA), F32), jax.ShapeDtypeStruct((m, QK_ROPE), F32)]
    if transposed:
        out_specs += [pl.BlockSpec((MLA_HEADS, tm, MLA_QK), lambda i: (0, i, 0)),
                      pl.BlockSpec((MLA_HEADS, V_HEAD, tm), lambda i: (0, 0, i))]
        out_shape += [jax.ShapeDtypeStruct((MLA_HEADS, m, MLA_QK), BF16),
                      jax.ShapeDtypeStruct((MLA_HEADS, V_HEAD, m), BF16)]
    return pl.pallas_call(
        functools.partial(_mla_proj_kernel, transposed=transposed),
        grid=(m // tm,),
        in_specs=[
            pl.BlockSpec((tm, D_MODEL), row),
            _resident((1, D_MODEL)),
            _resident(wa.shape),
            _resident((1, Q_LORA)),
            _resident((1, KV_LORA)),
            _resident(wuq.shape),
            _resident(wuk.shape),
            _resident(wuvt.shape),
            pl.BlockSpec((tm, LANE), lambda i: (i % nt, 0)),
            q_tab,
            q_tab,
        ],
        out_specs=out_specs,
        out_shape=out_shape,
        compiler_params=_params(("parallel",)),
        name="mla_proj",
    )(x, g, wa, qg, kvg, wuq, wuk, wuvt, krt, cq, sq)


MM_SUB = 512


def _mm_kernel(*refs, epilogue, tn):
    if epilogue == "rope":
        u_ref, w_ref, c_ref, s_ref, o_ref = refs
    else:
        u_ref, w_ref, o_ref = refs
    u = u_ref[...]
    for n in range(tn // MM_SUB):
        cols = slice(n * MM_SUB, (n + 1) * MM_SUB)
        acc = jnp.dot(u, w_ref[:, cols], preferred_element_type=F32)
        if epilogue == "rope":
            is_k = pl.program_id(1) * tn + n * MM_SUB >= RET_QK_W
            scale = jnp.where(is_k, RET_K_SCALE, 1.0).astype(F32)
            c = c_ref[...] * scale
            s = s_ref[...] * scale
            for h in range(MM_SUB // RET_DK):
                xh = acc[:, h * RET_DK:(h + 1) * RET_DK]
                lo = n * MM_SUB + h * RET_DK
                o_ref[:, lo:lo + RET_DK] = (xh * c + pltpu.roll(xh, RET_DK // 2, 1) * s).astype(BF16)
        elif epilogue == "silu":
            o_ref[:, cols] = (acc * jax.nn.sigmoid(acc)).astype(BF16)
        elif epilogue == "sigmoid":
            o_ref[:, cols] = jax.nn.sigmoid(acc).astype(BF16)
        else:
            o_ref[:, cols] = acc.astype(BF16)


def _mm(u, w, col0, ncols, epilogue, tables=None, *, tm, tn):
    m = u.shape[0]
    c0 = col0 // tn
    in_specs = [
        pl.BlockSpec((tm, D_MODEL), lambda i, j: (i, 0)),
        pl.BlockSpec((D_MODEL, tn), lambda i, j: (0, c0 + j)),
    ]
    args = [u, w]
    if epilogue == "rope":
        nt = tables[0].shape[0] // tm
        in_specs += [pl.BlockSpec((tm, LANE), lambda i, j: (i % nt, 0))] * 2
        args += list(tables)
    return pl.pallas_call(
        functools.partial(_mm_kernel, epilogue=epilogue, tn=tn),
        grid=(m // tm, ncols // tn),
        in_specs=in_specs,
        out_specs=pl.BlockSpec((tm, tn), lambda i, j: (i, j)),
        out_shape=jax.ShapeDtypeStruct((m, ncols), BF16),
        compiler_params=_params(("parallel", "arbitrary")),
        name="mm_" + epilogue,
    )(*args)


def _attn_kernel(qt_ref, k_ref, vt_ref, bias_ref, o_ref, m_ref, l_ref, acc_ref, s_ref, p_ref, a_ref, *, tq, tk, nq):
    assert tq == 2 * tk
    for i in range(nq):
        st = i % 2
        qt = qt_ref[:, i * tq:(i + 1) * tq]
        m_ref[st] = jnp.full((1, tq), -jnp.inf, F32)
        l_ref[st] = jnp.zeros((1, tq), F32)
        acc_ref[st] = jnp.zeros((V_HEAD, tq), F32)
        p_ref[st, 1] = jnp.zeros((tk, tq), BF16)
        a_ref[st, 1] = jnp.ones((1, tq), F32)

        def scores(j, slot):
            s_ref[st, slot] = jnp.dot(k_ref[j * tk:(j + 1) * tk, :], qt, preferred_element_type=F32)

        def softmax(slot, bias):
            s = s_ref[st, slot]
            if bias is not None:
                s = s + bias
            m = m_ref[st]
            m_new = jnp.maximum(m, jnp.max(s, axis=0, keepdims=True))
            alpha = jnp.exp2(m - m_new)
            p = jnp.exp2(s - m_new)
            l_ref[st] = alpha * l_ref[st] + jnp.sum(p, axis=0, keepdims=True)
            m_ref[st] = m_new
            p_ref[st, slot] = p.astype(BF16)
            a_ref[st, slot] = alpha

        def pv(j, slot):
            acc_ref[st] = a_ref[st, slot] * acc_ref[st] + jnp.dot(
                vt_ref[:, j * tk:(j + 1) * tk], p_ref[st, slot], preferred_element_type=F32)

        scores(0, 0)
        for t in range(i):
            scores(2 * t + 1, 1)
            softmax(0, None)
            pv(max(2 * t - 1, 0), 1)
            scores(2 * t + 2, 0)
            softmax(1, None)
            pv(2 * t, 0)
        hi = slice(tk, tq)
        j1 = 2 * i + 1
        s_hi = jnp.dot(k_ref[j1 * tk:(j1 + 1) * tk, :], qt[:, hi], preferred_element_type=F32) + bias_ref[1][:, hi]
        softmax(0, bias_ref[0])
        pv(max(2 * i - 1, 0), 1)
        m = m_ref[st][:, hi]
        m_new = jnp.maximum(m, jnp.max(s_hi, axis=0, keepdims=True))
        alpha = jnp.exp2(m - m_new)
        p_hi = jnp.exp2(s_hi - m_new)
        l_ref[st, :, hi] = alpha * l_ref[st][:, hi] + jnp.sum(p_hi, axis=0, keepdims=True)
        pv(2 * i, 0)
        acc_ref[st, :, hi] = alpha * acc_ref[st][:, hi] + jnp.dot(
            vt_ref[:, j1 * tk:(j1 + 1) * tk], p_hi.astype(BF16), preferred_element_type=F32)
        o_ref[i * tq:(i + 1) * tq, :] = (acc_ref[st] * (1.0 / l_ref[st])).T.astype(BF16)


def _attn_bias(tq, tk):
    key_chunk = jnp.arange(tq)[:, None] // CHUNK
    query_chunk = jnp.arange(tq)[None, :] // CHUNK
    bias = jnp.where(key_chunk <= query_chunk, 0.0, -jnp.inf).astype(F32)
    return bias.reshape(tq // tk, tk, tq)


def _attn(qt, k, vt, *, batch, seq, tq, tk):
    nq = seq // tq
    bias = _attn_bias(tq, tk)
    return pl.pallas_call(
        functools.partial(_attn_kernel, tq=tq, tk=tk, nq=nq),
        grid=(batch, MLA_HEADS),
        in_specs=[
            pl.BlockSpec((None, MLA_QK, seq), lambda b, h: (h, 0, b)),
            pl.BlockSpec((None, seq, MLA_QK), lambda b, h: (h, b, 0)),
            pl.BlockSpec((None, V_HEAD, seq), lambda b, h: (h, 0, b)),
            _resident(bias.shape),
        ],
        out_specs=pl.BlockSpec((seq, V_HEAD), lambda b, h: (b, h)),
        out_shape=jax.ShapeDtypeStruct((batch * seq, MLA_HEADS * V_HEAD), BF16),
        scratch_shapes=[pltpu.VMEM((2, 1, tq), F32), pltpu.VMEM((2, 1, tq), F32), pltpu.VMEM((2, V_HEAD, tq), F32),
                        pltpu.VMEM((2, 2, tk, tq), F32), pltpu.VMEM((2, 2, tk, tq), BF16),
                        pltpu.VMEM((2, 2, 1, tq), F32)],
        compiler_params=_params(("parallel", "arbitrary")),
        name="attn_prompt",
    )(qt, k, vt, bias)


def _attn_sample_kernel(q_ref, cn_ref, krn_ref, cc_ref, krc_ref, wukt_ref, wuv_ref, o_ref, ql_ref, qr_ref,
                        *, tq):
    for h in range(MLA_HEADS):
        qh = q_ref[h]
        ql_ref[h * tq:(h + 1) * tq, :] = jnp.dot(
            qh[:, :QK_NOPE], wukt_ref[h], preferred_element_type=F32).astype(BF16)
        qr_ref[h * tq:(h + 1) * tq, :] = qh[:, QK_NOPE:]
    ql = ql_ref[...]
    qr = qr_ref[...]
    cc = cc_ref[...].astype(BF16)
    krc_t = krc_ref[...].astype(BF16)
    cn = cn_ref[...].astype(BF16)
    krn = krn_ref[...].astype(BF16)
    s_c = (lax.dot_general(ql, cc, _NT, preferred_element_type=F32)
           + jnp.dot(qr, krc_t, preferred_element_type=F32))
    s_n = (lax.dot_general(ql, cn, _NT, preferred_element_type=F32)
           + lax.dot_general(qr, krn, _NT, preferred_element_type=F32))
    m = jnp.maximum(jnp.max(s_c, axis=-1, keepdims=True), jnp.max(s_n, axis=-1, keepdims=True))
    p_c = jnp.exp2(s_c - m)
    p_n = jnp.exp2(s_n - m)
    l = jnp.sum(p_c, axis=-1, keepdims=True) + jnp.sum(p_n, axis=-1, keepdims=True)
    ol = (jnp.dot(p_c.astype(BF16), cc, preferred_element_type=F32)
          + jnp.dot(p_n.astype(BF16), cn, preferred_element_type=F32)) / l
    ol = ol.astype(BF16)
    for h in range(MLA_HEADS):
        o_ref[:, h * V_HEAD:(h + 1) * V_HEAD] = jnp.dot(
            ol[h * tq:(h + 1) * tq], wuv_ref[h], preferred_element_type=F32).astype(BF16)


def _attn_sample(q, c_new, kr_new, cache_c, cache_kr, layer, wukt, wuv, *, batch, tq):
    past = cache_c.shape[2]
    return pl.pallas_call(
        functools.partial(_attn_sample_kernel, tq=tq),
        grid=(batch,),
        in_specs=[
            pl.BlockSpec((MLA_HEADS, tq, MLA_QK), lambda b: (0, b, 0)),
            pl.BlockSpec((tq, KV_LORA), lambda b: (b, 0)),
            pl.BlockSpec((tq, QK_ROPE), lambda b: (b, 0)),
            pl.BlockSpec((None, None, past, KV_LORA), lambda b: (layer, b, 0, 0)),
            pl.BlockSpec((None, None, QK_ROPE, past), lambda b: (layer, b, 0, 0)),
            _resident(wukt.shape),
            _resident(wuv.shape),
        ],
        out_specs=pl.BlockSpec((tq, MLA_HEADS * V_HEAD), lambda b: (b, 0)),
        out_shape=jax.ShapeDtypeStruct((batch * tq, MLA_HEADS * V_HEAD), BF16),
        scratch_shapes=[pltpu.VMEM((MLA_HEADS * tq, KV_LORA), BF16), pltpu.VMEM((MLA_HEADS * tq, QK_ROPE), BF16)],
        compiler_params=_params(("parallel",)),
        name="attn_sample",
    )(q, c_new, kr_new, cache_c, cache_kr, wukt, wuv)


def _ret_kernel(*refs, seq, cs, hps, has_state):
    if has_state:
        q_ref, k_ref, v_ref, dm_ref, qd_ref, kd_ref, cd_ref, s0_ref, o_ref, s_out = refs
    else:
        q_ref, k_ref, v_ref, dm_ref, qd_ref, kd_ref, cd_ref, o_ref, s_out = refs
    states = [s0_ref[h] if has_state else jnp.zeros((RET_DK, RET_DV), F32) for h in range(hps)]
    for c in range(seq // cs):
        rows = slice(c * cs, (c + 1) * cs)
        for h in range(hps):
            kcol = slice(h * RET_DK, (h + 1) * RET_DK)
            vcol = slice(h * RET_DV, (h + 1) * RET_DV)
            qc = q_ref[rows, kcol]
            kc = k_ref[rows, kcol]
            vc = v_ref[rows, vcol]
            att = lax.dot_general(qc, kc, _NT, preferred_element_type=F32) * dm_ref[h]
            o = (jnp.dot(att.astype(BF16), vc, preferred_element_type=F32)
                 + jnp.dot(qc, states[h].astype(BF16), preferred_element_type=F32) * qd_ref[h])
            kdec = (kc.astype(F32) * kd_ref[h]).astype(BF16)
            states[h] = states[h] * cd_ref[h] + lax.dot_general(kdec, vc, _TN, preferred_element_type=F32)
            mu = jnp.mean(o, axis=-1, keepdims=True)
            d = o - mu
            var = jnp.mean(d * d, axis=-1, keepdims=True)
            o_ref[rows, vcol] = (d * lax.rsqrt(var + GN_EPS)).astype(BF16)
    for h in range(hps):
        s_out[h] = states[h]


def _ret_tables(cs):
    lg = jnp.log1p(-jnp.exp2(-5.0 - jnp.arange(RET_HEADS, dtype=F32)))
    j = jnp.arange(cs, dtype=F32)
    diff = j[:, None] - j[None, :]
    dmask = jnp.where(diff[None] >= 0, jnp.exp(jnp.maximum(diff, 0.0)[None] * lg[:, None, None]), 0.0)
    q_dec = jnp.exp((j + 1.0)[None] * lg[:, None])[..., None]
    k_dec = jnp.exp((cs - 1.0 - j)[None] * lg[:, None])[..., None]
    c_dec = jnp.exp(cs * lg)[:, None, None]
    return (dmask,
            jnp.broadcast_to(q_dec, (RET_HEADS, cs, RET_DV)),
            jnp.broadcast_to(k_dec, (RET_HEADS, cs, RET_DK)),
            jnp.broadcast_to(c_dec, (RET_HEADS, 1, RET_DV)))


def _retention(qk, v, tables, state, layer, *, batch, seq, cs, hps):
    dmask, q_dec, k_dec, c_dec = tables
    k_col = RET_HEADS // hps
    head = lambda b, h: (h, 0, 0)
    in_specs = [
        pl.BlockSpec((seq, hps * RET_DK), lambda b, h: (b, h)),
        pl.BlockSpec((seq, hps * RET_DK), lambda b, h: (b, k_col + h)),
        pl.BlockSpec((seq, hps * RET_DV), lambda b, h: (b, h)),
        pl.BlockSpec((hps, cs, cs), head),
        pl.BlockSpec((hps, cs, RET_DV), head),
        pl.BlockSpec((hps, cs, RET_DK), head),
        pl.BlockSpec((hps, 1, RET_DV), head),
    ]
    args = [qk, qk, v, dmask, q_dec, k_dec, c_dec]
    if state is not None:
        in_specs.append(pl.BlockSpec((None, None, hps, RET_DK, RET_DV), lambda b, h: (layer, b, h, 0, 0)))
        args.append(state)
    return pl.pallas_call(
        functools.partial(_ret_kernel, seq=seq, cs=cs, hps=hps, has_state=state is not None),
        grid=(batch, RET_HEADS // hps),
        in_specs=in_specs,
        out_specs=[
            pl.BlockSpec((seq, hps * RET_DV), lambda b, h: (b, h)),
            pl.BlockSpec((None, hps, RET_DK, RET_DV), lambda b, h: (b, h, 0, 0)),
        ],
        out_shape=[
            jax.ShapeDtypeStruct((batch * seq, RET_V_W), BF16),
            jax.ShapeDtypeStruct((batch, RET_HEADS, RET_DK, RET_DV), F32),
        ],
        compiler_params=_params(("parallel", "arbitrary")),
        name="retention",
    )(*args)


def _merge_kernel(a_ref, ro_ref, rg_ref, gm_ref, gr_ref, x_ref, wm_ref, wr_ref, wo_ref, o_ref):
    a = jnp.dot(a_ref[...], wm_ref[...], preferred_element_type=F32)
    r = jnp.dot(rg_ref[...] * ro_ref[...], wr_ref[...], preferred_element_type=F32)
    mix = (gm_ref[...].astype(F32) * a + gr_ref[...].astype(F32) * r).astype(BF16)
    o_ref[...] = x_ref[...] + jnp.dot(mix, wo_ref[...], preferred_element_type=F32)


def _merge(a, ro, rg, gates, x, wm, wr, wo, *, tm):
    m = x.shape[0]
    row = lambda i: (i, 0)
    return pl.pallas_call(
        _merge_kernel,
        grid=(m // tm,),
        in_specs=[
            pl.BlockSpec((tm, MLA_HEADS * V_HEAD), row),
            pl.BlockSpec((tm, RET_V_W), row),
            pl.BlockSpec((tm, RET_V_W), row),
            pl.BlockSpec((tm, D_MODEL), lambda i: (i, 0)),
            pl.BlockSpec((tm, D_MODEL), lambda i: (i, 1)),
            pl.BlockSpec((tm, D_MODEL), row),
            _resident(wm.shape),
            _resident(wr.shape),
            _resident(wo.shape),
        ],
        out_specs=pl.BlockSpec((tm, D_MODEL), row),
        out_shape=jax.ShapeDtypeStruct((m, D_MODEL), F32),
        compiler_params=_params(("parallel",)),
        name="merge",
    )(a, ro, rg, gates, gates, x, wm, wr, wo)


def _rope_tables(pos, d):
    inv = ROPE_THETA ** (-jnp.arange(0, d, 2, dtype=F32) / d)
    ang = pos.astype(F32)[:, None] * inv[None, :]
    cos, sin = jnp.cos(ang), jnp.sin(ang)
    return jnp.concatenate([cos, cos], axis=-1), jnp.concatenate([-sin, sin], axis=-1)


def _half_swap(w):
    d = w.shape[-1]
    return jnp.concatenate([w[..., d // 2:], w[..., :d // 2]], axis=-1)


def _layer_weights(l, w_in, w_uq, w_uk, w_uv, w_mla_out, w_ret_out, w_out):
    wi = w_in[l]
    w_kr = wi[:, Q_LORA + KV_LORA:MLA_IN_W]
    wa = jnp.concatenate([wi[:, :MLA_IN_W], _half_swap(w_kr)], axis=1).astype(BF16)
    wq = w_uq[l].reshape(Q_LORA, MLA_HEADS, MLA_QK)
    wq_r = wq[:, :, QK_NOPE:]
    wuq = jnp.concatenate([
        wq[:, :, :QK_NOPE].reshape(Q_LORA, -1),
        wq_r.reshape(Q_LORA, -1),
        _half_swap(wq_r).reshape(Q_LORA, -1)], axis=1).astype(BF16)
    wuk = w_uk[l].reshape(KV_LORA, -1).astype(BF16)
    wuv = w_uv[l].reshape(KV_LORA, -1).astype(BF16)
    return dict(
        wa=wa, wrest=wi[:, MLA_IN_W:].astype(BF16), wuq=wuq, wuqt=wuq.T, wuk=wuk, wuvt=wuv.T,
        wukt=jnp.transpose(w_uk[l], (1, 2, 0)).astype(BF16),
        wuv=jnp.transpose(w_uv[l], (1, 0, 2)).astype(BF16),
        w_mla_out=w_mla_out[l].astype(BF16), w_ret_out=w_ret_out[l].astype(BF16), w_out=w_out[l].astype(BF16),
    )


def _tables(pos, transposed):
    c64, s64 = _rope_tables(pos, QK_ROPE)
    c128, s128 = _rope_tables(pos, RET_DK)
    tab = dict(krt=jnp.concatenate([c64, s64], axis=1), c128=c128, s128=s128)
    if transposed:
        tab.update(cq=c64.T, sq=s64.T)
    else:
        tab.update(cq=jnp.concatenate([c64, c64], axis=1), sq=jnp.concatenate([s64, s64], axis=1))
    return tab


def _mixer(h, w, g_mix, g_q, g_kv, tab, *, tm_proj, tm_mm, transposed):
    wuq = w["wuqt"] if transposed else w["wuq"]
    outs = _mla_proj(h, g_mix, w["wa"], g_q, g_kv, wuq, w["wuk"], w["wuvt"], tab["krt"], tab["cq"], tab["sq"],
                     tm=tm_proj, transposed=transposed)
    u = outs[0]
    tn = 2048
    wr = w["wrest"]
    qk = _mm(u, wr, 0, 2 * RET_QK_W, "rope", (tab["c128"], tab["s128"]), tm=tm_mm, tn=tn)
    v = _mm(u, wr, 2 * RET_QK_W, RET_V_W, "none", tm=tm_mm, tn=tn)
    rg = _mm(u, wr, 2 * RET_QK_W + RET_V_W, RET_V_W, "silu", tm=tm_mm, tn=tn)
    gates = _mm(u, wr, 2 * RET_QK_W + 2 * RET_V_W, 2 * D_MODEL, "sigmoid", tm=tm_mm, tn=tn)
    return outs[1:], qk, v, rg, gates


def kernel(x_prompt, x_sample, cache_ckv, cache_krope, state_ret, ffn1_norm, ffn1_w13, ffn1_w2, mix_norm, w_in,
           q_norm, kv_norm, w_uq, w_uk, w_uv, w_mla_out, w_ret_out, w_out, ffn2_norm, ffn2_w13, ffn2_w2,
           final_norm):
    batch, seq, _ = x_prompt.shape
    dbatch, dseq, _ = x_sample.shape
    past = cache_ckv.shape[2]
    mp, ms = batch * seq, dbatch * dseq

    tab_p = _tables(jnp.arange(seq), True)
    tab_s = _tables(jnp.tile(past + jnp.arange(dseq), dbatch), False)
    ret_tab_p = _ret_tables(256)
    ret_tab_s = _ret_tables(dseq)
    cache_krope_t = jnp.swapaxes(cache_krope, 2, 3)

    hp = x_prompt.reshape(mp, D_MODEL)
    hs = x_sample.reshape(ms, D_MODEL)
    outs = {k: [] for k in ("ckv_p", "kr_p", "ret_p", "ckv_s", "kr_s", "ret_s")}
    row = lambda v: v.reshape(1, -1)

    gf = row(final_norm)
    for l in range(DEPTH):
        w = _layer_weights(l, w_in, w_uq, w_uk, w_uv, w_mla_out, w_ret_out, w_out)
        g1, gm, gq, gkv, g2 = (row(ffn1_norm[l]), row(mix_norm[l]), row(q_norm[l]), row(kv_norm[l]),
                               row(ffn2_norm[l]))
        g_last = gf if l == DEPTH - 1 else None

        hs, w1, w3, w2 = _ffn(hs, g1, ffn1_w13, ffn1_w13, ffn1_w2, tm=ms, tf=512, cast_layer=l)
        hp = _ffn(hp, g1, w1, w3, w2, tm=1024, tf=512)

        (q, ckv, kr), qk, v, rg, gates = _mixer(hs, w, gm, gq, gkv, tab_s,
                                                tm_proj=ms, tm_mm=ms, transposed=False)
        a = _attn_sample(q, ckv, kr, cache_ckv, cache_krope_t, l, w["wukt"], w["wuv"], batch=dbatch, tq=dseq)
        ro, s_new = _retention(qk, v, ret_tab_s, state_ret, l, batch=dbatch, seq=dseq, cs=dseq, hps=RET_HEADS)
        hs = _merge(a, ro, rg, gates, hs, w["w_mla_out"], w["w_ret_out"], w["w_out"], tm=ms)
        outs["ckv_s"].append(ckv.reshape(dbatch, dseq, KV_LORA))
        outs["kr_s"].append(kr.reshape(dbatch, dseq, QK_ROPE))
        outs["ret_s"].append(s_new)

        (qt, ckv, kr, k, vt), qk, v, rg, gates = _mixer(hp, w, gm, gq, gkv, tab_p,
                                                        tm_proj=512, tm_mm=1024, transposed=True)
        a = _attn(qt, k, vt, batch=batch, seq=seq, tq=512, tk=256)
        ro, s_new = _retention(qk, v, ret_tab_p, None, l, batch=batch, seq=seq, cs=256, hps=2)
        hp = _merge(a, ro, rg, gates, hp, w["w_mla_out"], w["w_ret_out"], w["w_out"], tm=256)
        outs["ckv_p"].append(ckv.reshape(batch, seq, KV_LORA))
        outs["kr_p"].append(kr.reshape(batch, seq, QK_ROPE))
        outs["ret_p"].append(s_new)

        hs, w1, w3, w2 = _ffn(hs, g2, ffn2_w13, ffn2_w13, ffn2_w2, g_last, tm=ms, tf=512, cast_layer=l)
        hp = _ffn(hp, g2, w1, w3, w2, g_last, tm=1024, tf=512)

    y_prompt = hp.reshape(batch, seq, D_MODEL)
    y_sample = hs.reshape(dbatch, dseq, D_MODEL)
    return (y_prompt, y_sample, jnp.stack(outs["ckv_p"]), jnp.stack(outs["kr_p"]), jnp.stack(outs["ret_p"]),
            jnp.stack(outs["ckv_s"]), jnp.stack(outs["kr_s"]), jnp.stack(outs["ret_s"]))
```
